```python
import jax, jax.numpy as jnp
from jax import lax
import numpy as np

D_MODEL = 4096
BATCH = 4
SEQ = 2048
DEPTH = 1

GRID_W = 64
CTX_LEN = 256
HEAD_DIM = 128
N_HEADS = D_MODEL // HEAD_DIM
N_KV_HEADS = N_HEADS // 4
GROUP = N_HEADS // N_KV_HEADS
ATT_WIDTH = N_HEADS * HEAD_DIM
KV_WIDTH = N_KV_HEADS * HEAD_DIM
Q_BLOCK = 128
ROPE_THETA = 10000.0
AXIS_DIM = HEAD_DIM // 2
HG_EXPAND = 128
HG_HEADS = D_MODEL // HG_EXPAND
HG_DK = HG_EXPAND
HG_DV = D_MODEL // HG_HEADS
HG_KEY_WIDTH = HG_HEADS * HG_DK
HG_VAL_WIDTH = HG_HEADS * HG_DV
CHUNK = 32
EPS = 1e-6

COL_WIDTHS = (KV_WIDTH, KV_WIDTH,
              HG_VAL_WIDTH, HG_KEY_WIDTH, HG_KEY_WIDTH,
              ATT_WIDTH, ATT_WIDTH,
              HG_KEY_WIDTH, HG_VAL_WIDTH,
              D_MODEL, D_MODEL)
N_CTX_PARTS = 5
N_IN = sum(COL_WIDTHS)

kernel_name = "hybrid_hgrn2_gqa_prefix_dit_block"


def rms_norm(x, w):
    xf = x.astype(jnp.float32)
    y = xf * lax.rsqrt(jnp.mean(xf * xf, axis=-1, keepdims=True) + EPS)
    return (y * w.astype(jnp.float32)).astype(x.dtype)


def modulate(h, shift, scale):
    return h * (1 + scale) + shift


def split_cols(p, n_parts):
    points = np.cumsum(np.array(COL_WIDTHS[:n_parts]))[:-1].tolist()
    return jnp.split(p, points, axis=-1)


def axial_rope_tables(rows):
    row = jnp.repeat(jnp.arange(rows, dtype=jnp.float32), GRID_W)
    col = jnp.tile(jnp.arange(GRID_W, dtype=jnp.float32), rows)
    inv = ROPE_THETA ** (-jnp.arange(0, AXIS_DIM, 2, dtype=jnp.float32) / AXIS_DIM)
    fr = row[:, None] * inv[None]
    fc = col[:, None] * inv[None]
    ang = jnp.concatenate([fr, fr, fc, fc], axis=-1)
    return jnp.cos(ang), jnp.sin(ang)


def apply_rope(x, cos, sin):
    shape = (x.shape[1],) + (1,) * (x.ndim - 3) + (HEAD_DIM,)
    cos = cos.reshape(shape).astype(x.dtype)
    sin = sin.reshape(shape).astype(x.dtype)
    a, b, c, d = jnp.split(x, 4, axis=-1)
    rot = jnp.concatenate([-b, a, -d, c], axis=-1)
    return x * cos + rot * sin


def attend(qb, k, v):
    s = jnp.einsum('bqkgd,bskd->bkgqs', qb, k).astype(jnp.float32) * (HEAD_DIM ** -0.5)
    p = jax.nn.softmax(s, axis=-1).astype(v.dtype)
    return jnp.einsum('bkgqs,bskd->bqkgd', p, v)


def block_attention(q, k, v):
    B, T = q.shape[:2]
    qb = jnp.moveaxis(q.reshape(B, T // Q_BLOCK, Q_BLOCK, N_KV_HEADS, GROUP, HEAD_DIM), 1, 0)
    o = lax.map(lambda blk: attend(blk, k, v), qb)
    return jnp.moveaxis(o, 0, 1).reshape(B, T, ATT_WIDTH)


def hg_heads(a, d):
    return a.astype(jnp.float32).reshape(a.shape[:-1] + (HG_HEADS, d))


def hgrn_forget(z, lb):
    f = lb + (1.0 - lb) * jax.nn.sigmoid(z)
    return 1.0 - f, jnp.log(f)


def gla_chunked(q, k, v, g, s0):
    B, T, H, _ = k.shape
    n = T // CHUNK

    def chunks(a):
        return a.reshape(B, n, CHUNK, H, a.shape[-1])

    k, v, g = chunks(k), chunks(v), chunks(g)
    b = jnp.cumsum(g, axis=2)
    b_last = b[:, :, -1:]
    kd = k * jnp.exp(b_last - b)
    decay = jnp.exp(b_last[:, :, 0])
    if q is None:
        qe = None
        o_intra = None
    else:
        q = chunks(q)
        qe = q * jnp.exp(b)
        ke = k * jnp.exp(-b)
        mask = jnp.tril(jnp.ones((CHUNK, CHUNK), dtype=bool))
        A = jnp.where(mask, jnp.einsum('bnihk,bnjhk->bnhij', qe, ke), 0.0)
        o_intra = jnp.einsum('bnhij,bnjhv->bnihv', A, v)

    def step(S, xs):
        qe_n, kd_n, v_n, d_n = xs
        o = None if qe_n is None else jnp.einsum('bchk,bhkv->bchv', qe_n, S)
        S = d_n[..., None] * S + jnp.einsum('bchk,bchv->bhkv', kd_n, v_n)
        return S, o

    mv = lambda a: None if a is None else jnp.moveaxis(a, 1, 0)
    S, o_inter = lax.scan(step, s0, (mv(qe), mv(kd), mv(v), mv(decay)))
    if q is None:
        return None, S
    o = o_intra + jnp.moveaxis(o_inter, 0, 1)
    return o.reshape(B, T, H, -1), S


def hgrn2_scans(q, v, z_fwd, z_bwd, lb_fwd, lb_bwd, s_fwd, s_bwd):
    k_f, g_f = hgrn_forget(z_fwd, lb_fwd)
    k_b, g_b = hgrn_forget(z_bwd, lb_bwd)
    flip = lambda a: None if a is None else jnp.flip(a, axis=1)
    o_f, S_f = gla_chunked(q, k_f, v, g_f, s_fwd)
    o_b, S_b = gla_chunked(flip(q), flip(k_b), flip(v), flip(g_b), s_bwd)
    o = None if q is None else o_f + flip(o_b)
    return o, S_f, S_b


def branch_merge(o_hg, o_att, g_hg, g_att, r_hg, r_att, hg_norm_w, w_proj_hg, w_proj_att, w_out):
    dt = o_att.dtype
    o_hg = rms_norm(o_hg.astype(dt), hg_norm_w.reshape(HG_HEADS, HG_DV))
    y_hg = o_hg.reshape(o_hg.shape[:-2] + (HG_VAL_WIDTH,)) * jax.nn.silu(g_hg)
    y_att = o_att * jax.nn.silu(g_att)
    merged = jax.nn.sigmoid(r_hg) * (y_hg @ w_proj_hg) + jax.nn.sigmoid(r_att) * (y_att @ w_proj_att)
    return merged @ w_out


def setup_inputs(seed: int = 0) -> dict:
    key = jax.random.key(seed)
    ks = jax.random.split(key, 16)
    f32 = jnp.float32
    nrm = lambda k, s: jax.random.normal(k, s, f32)
    return {
        "x": nrm(ks[0], (BATCH, SEQ, D_MODEL)),
        "c": nrm(ks[1], (BATCH, D_MODEL)),
        "ctx": nrm(ks[2], (BATCH, CTX_LEN, D_MODEL)),
        "c_ctx": nrm(ks[3], (D_MODEL,)),
        "norm_w": 1.0 + 0.1 * nrm(ks[4], (DEPTH, D_MODEL)),
        "w_ada": 0.5 * D_MODEL ** -0.5 * nrm(ks[5], (DEPTH, D_MODEL, 3 * D_MODEL)),
        "b_ada": 0.01 * nrm(ks[6], (DEPTH, 3 * D_MODEL)),
        "w_in": D_MODEL ** -0.5 * nrm(ks[7], (DEPTH, D_MODEL, N_IN)),
        "q_norm_w": 1.0 + 0.1 * nrm(ks[8], (DEPTH, HEAD_DIM)),
        "k_norm_w": 1.0 + 0.1 * nrm(ks[9], (DEPTH, HEAD_DIM)),
        "lb_logits": 0.1 * nrm(ks[10], (2, DEPTH + 1, HG_KEY_WIDTH)),
        "hg_norm_w": 1.0 + 0.1 * nrm(ks[11], (DEPTH, HG_VAL_WIDTH)),
        "w_proj_hg": HG_VAL_WIDTH ** -0.5 * nrm(ks[12], (DEPTH, HG_VAL_WIDTH, D_MODEL)),
        "w_proj_att": ATT_WIDTH ** -0.5 * nrm(ks[13], (DEPTH, ATT_WIDTH, D_MODEL)),
        "w_out": D_MODEL ** -0.5 * nrm(ks[14], (DEPTH, D_MODEL, D_MODEL)),
        "final_norm_w": 1.0 + 0.1 * nrm(ks[15], (D_MODEL,)),
    }


def reference(x, c, ctx, c_ctx, norm_w, w_ada, b_ada, w_in, q_norm_w, k_norm_w, lb_logits,
              hg_norm_w, w_proj_hg, w_proj_att, w_out, final_norm_w):
    B, T, _ = x.shape
    L = ctx.shape[1]
    ROWS = T // GRID_W
    cos, sin = axial_rope_tables(ROWS)
    lb_all = jnp.cumsum(jax.nn.softmax(lb_logits.astype(jnp.float32), axis=1), axis=1)
    s_zero = jnp.zeros((B, HG_HEADS, HG_DK, HG_DV), jnp.float32)

    for l in range(DEPTH):
        last = l == DEPTH - 1
        shift, scale, gate = jnp.split(jax.nn.silu(c) @ w_ada[l] + b_ada[l], 3, axis=-1)
        shift_c, scale_c, gate_c = jnp.split(jax.nn.silu(c_ctx) @ w_ada[l] + b_ada[l], 3, axis=-1)
        h = modulate(rms_norm(x, norm_w[l]), shift[:, None], scale[:, None])
        h_c = modulate(rms_norm(ctx, norm_w[l]), shift_c, scale_c)

        (k_att, v_att, i_hg, f_fwd, f_bwd, q_att, g_att, q_hg, g_hg, r_hg, r_att) = split_cols(h @ w_in[l], len(COL_WIDTHS))
        n_ctx_parts = N_CTX_PARTS if last else len(COL_WIDTHS)
        w_in_c = w_in[l][:, :sum(COL_WIDTHS[:n_ctx_parts])]
        parts_c = split_cols(h_c @ w_in_c, n_ctx_parts)
        kc_att, vc_att, ic_hg, fc_fwd, fc_bwd = parts_c[:N_CTX_PARTS]

        lb_f = lb_all[0, l].reshape(HG_HEADS, HG_DK)
        lb_b = lb_all[1, l].reshape(HG_HEADS, HG_DK)

        qc_hg = None if last else jax.nn.silu(hg_heads(parts_c[7], HG_DK))
        oc_hg, S_cf, S_cb = hgrn2_scans(qc_hg, hg_heads(ic_hg, HG_DV), hg_heads(fc_fwd, HG_DK),
                                        hg_heads(fc_bwd, HG_DK), lb_f, lb_b, s_zero, s_zero)
        o_hg, _, _ = hgrn2_scans(jax.nn.silu(hg_heads(q_hg, HG_DK)), hg_heads(i_hg, HG_DV),
                                 hg_heads(f_fwd, HG_DK), hg_heads(f_bwd, HG_DK), lb_f, lb_b, S_cf, S_cb)

        qa = apply_rope(rms_norm(q_att.reshape(B, T, N_KV_HEADS, GROUP, HEAD_DIM), q_norm_w[l]), cos, sin)
        ka = apply_rope(rms_norm(k_att.reshape(B, T, N_KV_HEADS, HEAD_DIM), k_norm_w[l]), cos, sin)
        va = v_att.reshape(B, T, N_KV_HEADS, HEAD_DIM)
        kc = rms_norm(kc_att.reshape(B, L, N_KV_HEADS, HEAD_DIM), k_norm_w[l])
        vc = vc_att.reshape(B, L, N_KV_HEADS, HEAD_DIM)
        k_all = jnp.concatenate([kc, ka], axis=1)
        v_all = jnp.concatenate([vc, va], axis=1)
        o_att = block_attention(qa, k_all, v_all)

        out = branch_merge(o_hg, o_att, g_hg, g_att, r_hg, r_att,
                           hg_norm_w[l], w_proj_hg[l], w_proj_att[l], w_out[l])

        if not last:
            qc = rms_norm(parts_c[5].reshape(B, L, N_KV_HEADS, GROUP, HEAD_DIM), q_norm_w[l])
            oc_att = attend(qc, kc, vc).reshape(B, L, ATT_WIDTH)
            out_c = branch_merge(oc_hg, oc_att, parts_c[8], parts_c[6], parts_c[9], parts_c[10],
                                 hg_norm_w[l], w_proj_hg[l], w_proj_att[l], w_out[l])
            ctx = ctx + gate_c * out_c

        x = x + gate[:, None] * out

    return rms_norm(x, final_norm_w)
```

```python
import functools

import jax
import jax.numpy as jnp
from jax import lax
from jax.experimental import pallas as pl
from jax.experimental.pallas import tpu as pltpu

F32 = jnp.float32
BF16 = jnp.bfloat16

HEAD_DIM = 128
GROUP = 4
GRID_W = 64
ROPE_THETA = 10000.0
EPS = 1e-6
SUB = 32
LANES = 128
VMEM_LIMIT = 56 * 1024 * 1024


def _params(sem):
    return pltpu.CompilerParams(dimension_semantics=sem, vmem_limit_bytes=VMEM_LIMIT)


def _sigmoid(x):
    return 1.0 / (1.0 + jnp.exp(-x))


def _dot(a, b):
    return jnp.dot(a, b, preferred_element_type=F32)


def _dot_nt(a, b):
    return lax.dot_general(a, b, (((1,), (1,)), ((), ())), preferred_element_type=F32)


def _dot_tn(a, b):
    return lax.dot_general(a, b, (((0,), (0,)), ((), ())), preferred_element_type=F32)


def _split_bf16(x):
    hi = x.astype(BF16)
    lo = (x - hi.astype(F32)).astype(BF16)
    return hi, lo


def _ada_kernel(s_ref, w_ref, b_ref, o_ref):
    s = s_ref[...]
    s = s * _sigmoid(s)
    s_hi, s_lo = _split_bf16(s)
    w_hi, w_lo = _split_bf16(w_ref[...])
    rows = s.shape[0]
    r = _dot(jnp.concatenate([s_hi, s_lo], axis=0), w_hi)
    o_ref[...] = r[:rows] + r[rows:] + _dot(s_hi, w_lo) + b_ref[...]


def _ada(s, w, b, tn):
    rows, d = s.shape
    n = w.shape[1]
    return pl.pallas_call(
        _ada_kernel,
        grid=(n // tn,),
        in_specs=[pl.BlockSpec((rows, d), lambda j: (0, 0)),
                  pl.BlockSpec((d, tn), lambda j: (0, j)),
                  pl.BlockSpec((1, tn), lambda j: (0, j))],
        out_specs=pl.BlockSpec((rows, tn), lambda j: (0, j)),
        out_shape=jax.ShapeDtypeStruct((rows, n), F32),
        compiler_params=_params(("arbitrary",)),
        name="ada",
    )(s, w, b)


def _prenorm_kernel(x_ref, c_ref, nw_ref, mod_ref, h_ref, *, n_lat, d):
    def emit(src):
        x = src[...]
        y = x * lax.rsqrt(jnp.mean(x * x, axis=-1, keepdims=True) + EPS) * nw_ref[...]
        shift = mod_ref[0, :, 0:d]
        scale = mod_ref[0, :, d:2 * d]
        h_ref[...] = (y * (1.0 + scale) + shift).astype(BF16)

    i = pl.program_id(0)
    pl.when(i < n_lat)(lambda: emit(x_ref))
    pl.when(i >= n_lat)(lambda: emit(c_ref))


def _prenorm(x2, c2, nw, mod3, batch, tm):
    m_lat, d = x2.shape
    m_ctx = c2.shape[0]
    n_lat, n_ctx = m_lat // tm, m_ctx // tm
    per_b = n_lat // batch
    return pl.pallas_call(
        functools.partial(_prenorm_kernel, n_lat=n_lat, d=d),
        grid=(n_lat + n_ctx,),
        in_specs=[pl.BlockSpec((tm, d), lambda i: (jnp.minimum(i, n_lat - 1), 0)),
                  pl.BlockSpec((tm, d), lambda i: (jnp.maximum(i - n_lat, 0), 0)),
                  pl.BlockSpec((1, d), lambda i: (0, 0)),
                  pl.BlockSpec((1, 1, 3 * d), lambda i: (jnp.minimum(i // per_b, batch), 0, 0))],
        out_specs=pl.BlockSpec((tm, d), lambda i: (i, 0)),
        out_shape=jax.ShapeDtypeStruct((m_lat + m_ctx, d), BF16),
        compiler_params=_params(("arbitrary",)),
        name="prenorm",
    )(x2, c2, nw, mod3)


def _ep_plain(acc, extras, outs):
    outs[0][...] = acc.astype(BF16)


def _ep_silu(acc, extras, outs):
    outs[0][...] = (acc * _sigmoid(acc)).astype(BF16)


def _ep_sigmoid(acc, extras, outs):
    outs[0][...] = _sigmoid(acc).astype(BF16)


def _ep_normrope(acc, extras, outs, *, post_scale):
    nw_ref, cos_ref, sin_ref = extras
    nw = nw_ref[...]
    cos = cos_ref[...]
    sin = sin_ref[...]
    lane = lax.broadcasted_iota(jnp.int32, cos.shape, 1)
    take_up = (lane // (HEAD_DIM // 4)) % 2 == 0
    for hh in range(acc.shape[1] // HEAD_DIM):
        a = acc[:, hh * HEAD_DIM:(hh + 1) * HEAD_DIM]
        y = a * lax.rsqrt(jnp.mean(a * a, axis=-1, keepdims=True) + EPS) * nw
        rot = jnp.where(take_up, pltpu.roll(y, HEAD_DIM - HEAD_DIM // 4, 1),
                        pltpu.roll(y, HEAD_DIM // 4, 1))
        o = y * cos + rot * sin
        if post_scale != 1.0:
            o = o * post_scale
        outs[0][:, hh * HEAD_DIM:(hh + 1) * HEAD_DIM] = o.astype(BF16)


def _ep_forget(acc, extras, outs):
    lg = extras[0][...]
    mx = jnp.max(lg, axis=0, keepdims=True)
    e = jnp.exp(lg - mx)
    lb = e[0:1, :] / jnp.sum(e, axis=0, keepdims=True)
    sg = _sigmoid(acc)
    f = lb + (1.0 - lb) * sg
    outs[0][...] = jnp.log(f)
    outs[1][...] = (1.0 - f).astype(BF16)


def _inproj_kernel(h_ref, w_ref, *rest, epilogue, n_extra, cast_rows):
    extras = rest[:n_extra]
    outs = rest[n_extra:-1]
    wbf = rest[-1]

    @pl.when(pl.program_id(1) == 0)
    def _():
        def body(i, carry):
            r = pl.multiple_of(i * cast_rows, cast_rows)
            wbf[pl.ds(r, cast_rows), :] = w_ref[pl.ds(r, cast_rows), :].astype(BF16)
            return carry
        lax.fori_loop(0, wbf.shape[0] // cast_rows, body, 0)

    acc = _dot(h_ref[...], wbf[...])
    epilogue(acc, extras, outs)


def _inproj(h, w, col0, width, rows, epilogue, extras, extra_specs, out_dtypes, tm, tn, name):
    k = h.shape[1]
    grid = (width // tn, rows // tm)
    cb = col0 // tn
    in_specs = [pl.BlockSpec((tm, k), lambda n, m: (m, 0)),
                pl.BlockSpec((k, tn), lambda n, m: (0, cb + n))] + list(extra_specs)
    out_specs = [pl.BlockSpec((tm, tn), lambda n, m: (m, n)) for _ in out_dtypes]
    out_shape = [jax.ShapeDtypeStruct((rows, width), dt) for dt in out_dtypes]
    res = pl.pallas_call(
        functools.partial(_inproj_kernel, epilogue=epilogue, n_extra=len(extras),
                          cast_rows=min(256, k)),
        grid=grid,
        in_specs=in_specs,
        out_specs=out_specs,
        out_shape=out_shape,
        scratch_shapes=[pltpu.VMEM((k, tn), BF16)],
        compiler_params=_params(("arbitrary", "arbitrary")),
        name=name,
    )(h, w, *extras)
    return res


def _attn_kernel(q_ref, kl_ref, vl_ref, kc_ref, vc_ref, sg_ref, o_ref):
    kl = kl_ref[...]
    vl = vl_ref[...]
    kc = kc_ref[...]
    vc = vc_ref[...]
    for g in range(GROUP):
        cs = slice(g * HEAD_DIM, (g + 1) * HEAD_DIM)
        q = q_ref[:, cs]
        sl = _dot_nt(q, kl)
        sc = _dot_nt(q, kc)
        mx = jnp.maximum(jnp.max(sl, axis=-1, keepdims=True), jnp.max(sc, axis=-1, keepdims=True))
        pl_ = jnp.exp(sl - mx)
        pc = jnp.exp(sc - mx)
        den = jnp.sum(pl_, axis=-1, keepdims=True) + jnp.sum(pc, axis=-1, keepdims=True)
        o = (_dot(pl_.astype(BF16), vl) + _dot(pc.astype(BF16), vc)) / den
        o_ref[:, cs] = (o * sg_ref[:, cs].astype(F32)).astype(BF16)


def _attention(q, kk, vv, sg, batch, t_len, c_len, tq):
    m_lat, aw = q.shape
    n_kv = kk.shape[1] // HEAD_DIM
    gw = GROUP * HEAD_DIM
    nq = t_len // tq
    ctx0 = m_lat // c_len
    return pl.pallas_call(
        _attn_kernel,
        grid=(batch, n_kv, nq),
        in_specs=[pl.BlockSpec((tq, gw), lambda b, h, i: (b * nq + i, h)),
                  pl.BlockSpec((t_len, HEAD_DIM), lambda b, h, i: (b, h)),
                  pl.BlockSpec((t_len, HEAD_DIM), lambda b, h, i: (b, h)),
                  pl.BlockSpec((c_len, HEAD_DIM), lambda b, h, i: (ctx0 + b, h)),
                  pl.BlockSpec((c_len, HEAD_DIM), lambda b, h, i: (ctx0 + b, h)),
                  pl.BlockSpec((tq, gw), lambda b, h, i: (b * nq + i, h))],
        out_specs=pl.BlockSpec((tq, gw), lambda b, h, i: (b * nq + i, h)),
        out_shape=jax.ShapeDtypeStruct((m_lat, aw), BF16),
        compiler_params=_params(("arbitrary", "arbitrary", "arbitrary")),
        name="attn",
    )(q, kk, vv, kk, vv, sg)


def _hgrn_head(q, k, v, bfull, st, reverse, want_out):
    c = k.shape[0]
    ns = c // SUB
    order = list(range(ns))[::-1] if reverse else list(range(ns))
    zero = jnp.zeros((1, LANES), F32)
    start, end = {}, {}
    for i in range(ns):
        lo, hi = i * SUB, i * SUB + SUB - 1
        if reverse:
            start[i] = bfull[hi + 1:hi + 2, :] if i < ns - 1 else zero
            end[i] = bfull[lo:lo + 1, :]
        else:
            start[i] = bfull[lo - 1:lo, :] if i > 0 else zero
            end[i] = bfull[hi:hi + 1, :]
    btot = end[order[-1]]

    kd, o_intra = {}, {}
    for p, i in enumerate(order):
        rs = slice(i * SUB, (i + 1) * SUB)
        b_i = bfull[rs]
        kd[i] = k[rs] * jnp.exp(end[i] - b_i)
        if want_out:
            bl = b_i - start[i]
            qt = (q[rs] * jnp.exp(bl)).astype(BF16)
            kh = k[rs] * jnp.exp(-bl)
            earlier = order[:p]
            kparts = [kd[j] * jnp.exp(start[i] - end[j]) for j in earlier] + [kh]
            vparts = [v[j * SUB:(j + 1) * SUB] for j in earlier] + [v[rs]]
            kcat = jnp.concatenate(kparts, axis=0).astype(BF16)
            vcat = jnp.concatenate(vparts, axis=0)
            a = _dot_nt(qt, kcat)
            row = lax.broadcasted_iota(jnp.int32, a.shape, 0)
            col = lax.broadcasted_iota(jnp.int32, a.shape, 1) - p * SUB
            keep = (col < 0) | ((col >= row) if reverse else (col <= row))
            a = jnp.where(keep, a, 0.0)
            o_intra[i] = _dot(a.astype(BF16), vcat)

    o = None
    if want_out:
        qs = (q * jnp.exp(bfull)).astype(BF16)
        o = jnp.concatenate([o_intra[i] for i in range(ns)], axis=0) + _dot_nt(qs, st.astype(BF16))
    ke = jnp.concatenate([kd[i] * jnp.exp(btot - end[i]) for i in range(ns)], axis=0).astype(BF16)
    st_new = jnp.exp(btot) * st + _dot_tn(v, ke)
    return o, st_new


def _hgrn_kernel(*refs, reverse, n_ctx, heads, final):
    if final:
        q_ref, k_ref, g_ref, v_ref, of_ref, sg_ref, hw_ref, o_ref, st_ref = refs
    else:
        q_ref, k_ref, g_ref, v_ref, o_ref, st_ref = refs
    s = pl.program_id(2)

    @pl.when(s == 0)
    def _():
        st_ref[...] = jnp.zeros_like(st_ref)

    c = k_ref.shape[0]
    ii = lax.broadcasted_iota(jnp.int32, (c, c), 0)
    jj = lax.broadcasted_iota(jnp.int32, (c, c), 1)
    tri = jnp.where((jj >= ii) if reverse else (jj <= ii), 1.0, 0.0).astype(BF16)
    g_hi, g_lo = _split_bf16(g_ref[...])
    bfull_all = _dot(tri, g_hi) + _dot(tri, g_lo)

    def run(want_out):
        for j in range(heads):
            cs = slice(j * LANES, (j + 1) * LANES)
            q = q_ref[:, cs].astype(F32) if want_out else None
            o, st_new = _hgrn_head(q, k_ref[:, cs].astype(F32), v_ref[:, cs], bfull_all[:, cs],
                                   st_ref[j], reverse, want_out)
            st_ref[j] = st_new
            if want_out:
                if final:
                    tot = o + of_ref[:, cs]
                    y = tot * lax.rsqrt(jnp.mean(tot * tot, axis=-1, keepdims=True) + EPS)
                    y = y * hw_ref[:, cs]
                    o_ref[:, cs] = (y * sg_ref[:, cs].astype(F32)).astype(o_ref.dtype)
                else:
                    o_ref[:, cs] = o.astype(o_ref.dtype)

    pl.when(s < n_ctx)(lambda: run(False))
    pl.when(s >= n_ctx)(lambda: run(True))


def _hgrn(q, kk, gg, vv, batch, t_len, c_len, reverse, heads, blk, final_args=None):
    m_lat, w = q.shape
    hb = heads * LANES
    n_ctx, n_lat = c_len // blk, t_len // blk
    ctx0 = m_lat // blk

    def row_block(b, s):
        if reverse:
            cblk = ctx0 + b * n_ctx + (n_ctx - 1 - s)
            lblk = b * n_lat + (n_lat - 1 - (s - n_ctx))
        else:
            cblk = ctx0 + b * n_ctx + s
            lblk = b * n_lat + (s - n_ctx)
        return jnp.where(s < n_ctx, cblk, lblk)

    def lat_block(b, s):
        sl = jnp.maximum(s - n_ctx, 0)
        return b * n_lat + ((n_lat - 1 - sl) if reverse else sl)

    all_spec = pl.BlockSpec((blk, hb), lambda b, h, s: (row_block(b, s), h))
    lat_spec = pl.BlockSpec((blk, hb), lambda b, h, s: (lat_block(b, s), h))
    in_specs = [lat_spec, all_spec, all_spec, all_spec]
    args = [q, kk, gg, vv]
    final = final_args is not None
    if final:
        o_f, sg, hw = final_args
        in_specs += [lat_spec, lat_spec, pl.BlockSpec((1, hb), lambda b, h, s: (0, h))]
        args += [o_f, sg, hw]
    return pl.pallas_call(
        functools.partial(_hgrn_kernel, reverse=reverse, n_ctx=n_ctx, heads=heads, final=final),
        grid=(batch, w // hb, n_ctx + n_lat),
        in_specs=in_specs,
        out_specs=lat_spec,
        out_shape=jax.ShapeDtypeStruct((m_lat, w), BF16 if final else F32),
        scratch_shapes=[pltpu.VMEM((heads, LANES, LANES), F32)],
        compiler_params=_params(("arbitrary", "arbitrary", "arbitrary")),
        name="hgrn_bwd" if reverse else "hgrn_fwd",
    )(*args)


def _cast_kernel(w_ref, o_ref):
    o_ref[...] = w_ref[...].astype(BF16)


def _cast_bf16(w, tr):
    r, c = w.shape
    return pl.pallas_call(
        _cast_kernel,
        grid=(r // tr,),
        in_specs=[pl.BlockSpec((tr, c), lambda i: (i, 0))],
        out_specs=pl.BlockSpec((tr, c), lambda i: (i, 0)),
        out_shape=jax.ShapeDtypeStruct((r, c), BF16),
        compiler_params=_params(("arbitrary",)),
        name="cast_bf16",
    )(w)


def _proj_kernel(yh_ref, ya_ref, wh_ref, wa_ref, rh_ref, ra_ref, o_ref):
    a = _dot(yh_ref[...], wh_ref[...])
    b = _dot(ya_ref[...], wa_ref[...])
    o_ref[...] = (rh_ref[...].astype(F32) * a + ra_ref[...].astype(F32) * b).astype(BF16)


def _proj(yh, ya, wh, wa, rh, ra, tm, tn):
    m, k = yh.shape
    n = wh.shape[1]
    return pl.pallas_call(
        _proj_kernel,
        grid=(m // tm, n // tn),
        in_specs=[pl.BlockSpec((tm, k), lambda i, j: (i, 0)),
                  pl.BlockSpec((tm, k), lambda i, j: (i, 0)),
                  pl.BlockSpec((k, tn), lambda i, j: (0, j)),
                  pl.BlockSpec((k, tn), lambda i, j: (0, j)),
                  pl.BlockSpec((tm, tn), lambda i, j: (i, j)),
                  pl.BlockSpec((tm, tn), lambda i, j: (i, j))],
        out_specs=pl.BlockSpec((tm, tn), lambda i, j: (i, j)),
        out_shape=jax.ShapeDtypeStruct((m, n), BF16),
        compiler_params=_params(("arbitrary", "arbitrary")),
        name="proj",
    )(yh, ya, wh, wa, rh, ra)


def _out_kernel(mg_ref, w_ref, x_ref, gate_ref, o_ref):
    o_ref[...] = x_ref[...] + gate_ref[0] * _dot(mg_ref[...], w_ref[...])


def _outproj(mg, w, x2, mod3, t_len, tm, tn):
    m, k = mg.shape
    n = w.shape[1]
    gate0 = 2 * n // tn
    per_b = t_len // tm
    return pl.pallas_call(
        _out_kernel,
        grid=(m // tm, n // tn),
        in_specs=[pl.BlockSpec((tm, k), lambda i, j: (i, 0)),
                  pl.BlockSpec((k, tn), lambda i, j: (0, j)),
                  pl.BlockSpec((tm, tn), lambda i, j: (i, j)),
                  pl.BlockSpec((1, 1, tn), lambda i, j: (i // per_b, 0, gate0 + j))],
        out_specs=pl.BlockSpec((tm, tn), lambda i, j: (i, j)),
        out_shape=jax.ShapeDtypeStruct((m, n), F32),
        compiler_params=_params(("arbitrary", "arbitrary")),
        name="outproj",
    )(mg, w, x2, mod3)


def _fnorm_kernel(x_ref, w_ref, o_ref):
    x = x_ref[...]
    o_ref[...] = x * lax.rsqrt(jnp.mean(x * x, axis=-1, keepdims=True) + EPS) * w_ref[...]


def _fnorm(x2, w, tm):
    m, d = x2.shape
    return pl.pallas_call(
        _fnorm_kernel,
        grid=(m // tm,),
        in_specs=[pl.BlockSpec((tm, d), lambda i: (i, 0)), pl.BlockSpec((1, d), lambda i: (0, 0))],
        out_specs=pl.BlockSpec((tm, d), lambda i: (i, 0)),
        out_shape=jax.ShapeDtypeStruct((m, d), F32),
        compiler_params=_params(("arbitrary",)),
        name="fnorm",
    )(x2, w)


def _rope_tables(t_len, pad_rows):
    axis_dim = HEAD_DIM // 2
    rows = t_len // GRID_W
    row = jnp.repeat(jnp.arange(rows, dtype=F32), GRID_W)
    col = jnp.tile(jnp.arange(GRID_W, dtype=F32), rows)
    inv = ROPE_THETA ** (-jnp.arange(0, axis_dim, 2, dtype=F32) / axis_dim)
    fr = row[:, None] * inv[None]
    fc = col[:, None] * inv[None]
    ang = jnp.concatenate([fr, fr, fc, fc], axis=-1)
    cos, sin = jnp.cos(ang), jnp.sin(ang)
    lane = jnp.arange(HEAD_DIM)
    sign = jnp.where((lane // (HEAD_DIM // 4)) % 2 == 0, -1.0, 1.0).astype(F32)
    cos = jnp.concatenate([cos, jnp.ones((pad_rows, HEAD_DIM), F32)], axis=0)
    sin = jnp.concatenate([sin * sign, jnp.zeros((pad_rows, HEAD_DIM), F32)], axis=0)
    return cos, sin


def _tile(n, pref):
    t = min(n, pref)
    while n % t:
        t //= 2
    return t


def kernel(x, c, ctx, c_ctx, norm_w, w_ada, b_ada, w_in, q_norm_w, k_norm_w, lb_logits,
           hg_norm_w, w_proj_hg, w_proj_att, w_out, final_norm_w):
    batch, t_len, d = x.shape
    c_len = ctx.shape[1]
    depth = norm_w.shape[0]
    assert depth == 1, "single-layer block"
    att_w = d
    kv_w = d // GROUP
    m_lat, m_ctx = batch * t_len, batch * c_len
    m_all = m_lat + m_ctx

    x2 = x.reshape(m_lat, d)
    c2 = ctx.reshape(m_ctx, d)

    pad = (-(batch + 1)) % 8
    s = jnp.concatenate([c, c_ctx[None, :], jnp.zeros((pad, d), F32)], axis=0)
    mod = _ada(s, w_ada[0], b_ada[0][None, :], _tile(3 * d, 512))
    mod3 = mod.reshape(mod.shape[0], 1, 3 * d)

    h = _prenorm(x2, c2, norm_w[0][None, :], mod3, batch, _tile(c_len, 256))

    tm = _tile(m_ctx, 512)
    tn = _tile(kv_w, 512)
    n_lat_tiles = m_lat // tm
    per_seq = t_len // tm
    cos, sin = _rope_tables(t_len, tm)
    rope_specs = [pl.BlockSpec((1, HEAD_DIM), lambda n, m: (0, 0)),
                  pl.BlockSpec((tm, HEAD_DIM), lambda n, m: (jnp.where(m < n_lat_tiles, m % per_seq, per_seq), 0)),
                  pl.BlockSpec((tm, HEAD_DIM), lambda n, m: (jnp.where(m < n_lat_tiles, m % per_seq, per_seq), 0))]
    w = w_in[0]
    col = 0

    def part(width, rows, epilogue, extras=(), extra_specs=(), out_dtypes=(BF16,), name="inproj"):
        nonlocal col
        res = _inproj(h, w, col, width, rows, epilogue, list(extras), list(extra_specs),
                      list(out_dtypes), tm, tn, name)
        col += width
        return res

    (k_att,) = part(kv_w, m_all, functools.partial(_ep_normrope, post_scale=1.0),
                    (k_norm_w[0][None, :], cos, sin), rope_specs, name="inproj_k")
    (v_att,) = part(kv_w, m_all, _ep_plain, name="inproj_v")
    (i_hg,) = part(d, m_all, _ep_plain, name="inproj_i")
    lb_spec = [pl.BlockSpec((lb_logits.shape[1], tn), lambda n, m: (0, n))]
    g_f, k_f = part(d, m_all, _ep_forget, (lb_logits[0],), lb_spec, (F32, BF16), name="inproj_ff")
    g_b, k_b = part(d, m_all, _ep_forget, (lb_logits[1],), lb_spec, (F32, BF16), name="inproj_fb")
    (q_att,) = part(att_w, m_lat, functools.partial(_ep_normrope, post_scale=HEAD_DIM ** -0.5),
                    (q_norm_w[0][None, :], cos, sin), rope_specs, name="inproj_q")
    (sg_att,) = part(att_w, m_lat, _ep_silu, name="inproj_ga")
    (q_hg,) = part(d, m_lat, _ep_silu, name="inproj_qh")
    (sg_hg,) = part(d, m_lat, _ep_silu, name="inproj_gh")
    (sr_hg,) = part(d, m_lat, _ep_sigmoid, name="inproj_rh")
    (sr_att,) = part(d, m_lat, _ep_sigmoid, name="inproj_ra")

    y_att = _attention(q_att, k_att, v_att, sg_att, batch, t_len, c_len, _tile(t_len, 256))

    blk = _tile(c_len, 128)
    heads = min(4, d // LANES)
    o_f = _hgrn(q_hg, k_f, g_f, i_hg, batch, t_len, c_len, False, heads, blk)
    y_hg = _hgrn(q_hg, k_b, g_b, i_hg, batch, t_len, c_len, True, heads, blk,
                 final_args=(o_f, sg_hg, hg_norm_w[0][None, :]))

    tr = _tile(d, 256)
    wh = _cast_bf16(w_proj_hg[0], tr)
    wa = _cast_bf16(w_proj_att[0], tr)
    wo = _cast_bf16(w_out[0], tr)
    tm2 = _tile(t_len, 512)
    tn2 = _tile(d, 512)
    merged = _proj(y_hg, y_att, wh, wa, sr_hg, sr_att, tm2, tn2)
    xn = _outproj(merged, wo, x2, mod3, t_len, tm2, tn2)
    out = _fnorm(xn, final_norm_w[None, :], _tile(m_lat, 256))
    return out.reshape(batch, t_len, d)
```

```python
import functools

import jax
import jax.numpy as jnp
from jax import lax
from jax.experimental import pallas as pl
from jax.experimental.pallas import tpu as pltpu

F32 = jnp.float32
BF16 = jnp.bfloat16

HEAD_DIM = 128
GROUP = 4
GRID_W = 64
ROPE_THETA = 10000.0
EPS = 1e-6
SUB = 32
LANES = 128
VMEM_LIMIT = 56 * 1024 * 1024


def _params(sem):
    return pltpu.CompilerParams(dimension_semantics=sem, vmem_limit_bytes=VMEM_LIMIT)


def _sigmoid(x):
    return 1.0 / (1.0 + jnp.exp(-x))


def _dot(a, b):
    return jnp.dot(a, b, preferred_element_type=F32)


def _dot_nt(a, b):
    return lax.dot_general(a, b, (((1,), (1,)), ((), ())), preferred_element_type=F32)


def _dot_tn(a, b):
    return lax.dot_general(a, b, (((0,), (0,)), ((), ())), preferred_element_type=F32)


def _split_bf16(x):
    hi = x.astype(BF16)
    lo = (x - hi.astype(F32)).astype(BF16)
    return hi, lo


def _ada_kernel(s_ref, w_ref, b_ref, o_ref):
    s = s_ref[...]
    s = s * _sigmoid(s)
    s_hi, s_lo = _split_bf16(s)
    w_hi, w_lo = _split_bf16(w_ref[...])
    rows = s.shape[0]
    r = _dot(jnp.concatenate([s_hi, s_lo], axis=0), w_hi)
    o_ref[...] = r[:rows] + r[rows:] + _dot(s_hi, w_lo) + b_ref[...]


def _ada(s, w, b, tn):
    rows, d = s.shape
    n = w.shape[1]
    return pl.pallas_call(
        _ada_kernel,
        grid=(n // tn,),
        in_specs=[pl.BlockSpec((rows, d), lambda j: (0, 0)),
                  pl.BlockSpec((d, tn), lambda j: (0, j)),
                  pl.BlockSpec((1, tn), lambda j: (0, j))],
        out_specs=pl.BlockSpec((rows, tn), lambda j: (0, j)),
        out_shape=jax.ShapeDtypeStruct((rows, n), F32),
        compiler_params=_params(("arbitrary",)),
        name="ada",
    )(s, w, b)


def _prenorm_kernel(x_ref, c_ref, nw_ref, mod_ref, h_ref, *, n_lat, d):
    def emit(src):
        x = src[...]
        y = x * lax.rsqrt(jnp.mean(x * x, axis=-1, keepdims=True) + EPS) * nw_ref[...]
        shift = mod_ref[0, :, 0:d]
        scale = mod_ref[0, :, d:2 * d]
        h_ref[...] = (y * (1.0 + scale) + shift).astype(BF16)

    i = pl.program_id(0)
    pl.when(i < n_lat)(lambda: emit(x_ref))
    pl.when(i >= n_lat)(lambda: emit(c_ref))


def _prenorm(x2, c2, nw, mod3, batch, tm):
    m_lat, d = x2.shape
    m_ctx = c2.shape[0]
    n_lat, n_ctx = m_lat // tm, m_ctx // tm
    per_b = n_lat // batch
    return pl.pallas_call(
        functools.partial(_prenorm_kernel, n_lat=n_lat, d=d),
        grid=(n_lat + n_ctx,),
        in_specs=[pl.BlockSpec((tm, d), lambda i: (jnp.minimum(i, n_lat - 1), 0)),
                  pl.BlockSpec((tm, d), lambda i: (jnp.maximum(i - n_lat, 0), 0)),
                  pl.BlockSpec((1, d), lambda i: (0, 0)),
                  pl.BlockSpec((1, 1, 3 * d), lambda i: (jnp.minimum(i // per_b, batch), 0, 0))],
        out_specs=pl.BlockSpec((tm, d), lambda i: (i, 0)),
        out_shape=jax.ShapeDtypeStruct((m_lat + m_ctx, d), BF16),
        compiler_params=_params(("arbitrary",)),
        name="prenorm",
    )(x2, c2, nw, mod3)


def _ep_plain(acc, extras, outs):
    outs[0][...] = acc.astype(BF16)


def _ep_silu(acc, extras, outs):
    outs[0][...] = (acc * _sigmoid(acc)).astype(BF16)


def _ep_sigmoid(acc, extras, outs):
    outs[0][...] = _sigmoid(acc).astype(BF16)


def _ep_normrope(acc, extras, outs, *, post_scale):
    nw_ref, cos_ref, sin_ref = extras
    nw = nw_ref[...]
    cos = cos_ref[...]
    sin = sin_ref[...]
    lane = lax.broadcasted_iota(jnp.int32, cos.shape, 1)
    take_up = (lane // (HEAD_DIM // 4)) % 2 == 0
    for hh in range(acc.shape[1] // HEAD_DIM):
        a = acc[:, hh * HEAD_DIM:(hh + 1) * HEAD_DIM]
        y = a * lax.rsqrt(jnp.mean(a * a, axis=-1, keepdims=True) + EPS) * nw
        rot = jnp.where(take_up, pltpu.roll(y, HEAD_DIM - HEAD_DIM // 4, 1),
                        pltpu.roll(y, HEAD_DIM // 4, 1))
        o = y * cos + rot * sin
        if post_scale != 1.0:
            o = o * post_scale
        outs[0][:, hh * HEAD_DIM:(hh + 1) * HEAD_DIM] = o.astype(BF16)


def _ep_forget(acc, extras, outs):
    lg = extras[0][...]
    mx = jnp.max(lg, axis=0, keepdims=True)
    e = jnp.exp(lg - mx)
    lb = e[0:1, :] / jnp.sum(e, axis=0, keepdims=True)
    sg = _sigmoid(acc)
    f = lb + (1.0 - lb) * sg
    outs[0][...] = jnp.log(f)
    outs[1][...] = (1.0 - f).astype(BF16)


def _inproj_kernel(h_ref, w_ref, *rest, epilogue, n_extra, cast_rows):
    extras = rest[:n_extra]
    outs = rest[n_extra:-1]
    wbf = rest[-1]

    @pl.when(pl.program_id(1) == 0)
    def _():
        def body(i, carry):
            r = pl.multiple_of(i * cast_rows, cast_rows)
            wbf[pl.ds(r, cast_rows), :] = w_ref[pl.ds(r, cast_rows), :].astype(BF16)
            return carry
        lax.fori_loop(0, wbf.shape[0] // cast_rows, body, 0)

    acc = _dot(h_ref[...], wbf[...])
    epilogue(acc, extras, outs)


def _inproj(h, w, col0, width, rows, epilogue, extras, extra_specs, out_dtypes, tm, tn, name):
    k = h.shape[1]
    grid = (width // tn, rows // tm)
    cb = col0 // tn
    in_specs = [pl.BlockSpec((tm, k), lambda n, m: (m, 0)),
                pl.BlockSpec((k, tn), lambda n, m: (0, cb + n))] + list(extra_specs)
    out_specs = [pl.BlockSpec((tm, tn), lambda n, m: (m, n)) for _ in out_dtypes]
    out_shape = [jax.ShapeDtypeStruct((rows, width), dt) for dt in out_dtypes]
    res = pl.pallas_call(
        functools.partial(_inproj_kernel, epilogue=epilogue, n_extra=len(extras),
                          cast_rows=min(256, k)),
        grid=grid,
        in_specs=in_specs,
        out_specs=out_specs,
        out_shape=out_shape,
        scratch_shapes=[pltpu.VMEM((k, tn), BF16)],
        compiler_params=_params(("arbitrary", "arbitrary")),
        name=name,
    )(h, w, *extras)
    return res


def _attn_kernel(q_ref, kl_ref, vl_ref, kc_ref, vc_ref, sg_ref, o_ref):
    kl = kl_ref[...]
    vl = vl_ref[...]
    kc = kc_ref[...]
    vc = vc_ref[...]
    for g in range(GROUP):
        cs = slice(g * HEAD_DIM, (g + 1) * HEAD_DIM)
        q = q_ref[:, cs]
        sl = _dot_nt(q, kl)
        sc = _dot_nt(q, kc)
        mx = jnp.maximum(jnp.max(sl, axis=-1, keepdims=True), jnp.max(sc, axis=-1, keepdims=True))
        pl_ = jnp.exp(sl - mx)
        pc = jnp.exp(sc - mx)
        den = jnp.sum(pl_, axis=-1, keepdims=True) + jnp.sum(pc, axis=-1, keepdims=True)
        o = (_dot(pl_.astype(BF16), vl) + _dot(pc.astype(BF16), vc)) / den
        o_ref[:, cs] = (o * sg_ref[:, cs].astype(F32)).astype(BF16)


def _attention(q, kk, vv, sg, batch, t_len, c_len, tq):
    m_lat, aw = q.shape
    n_kv = kk.shape[1] // HEAD_DIM
    gw = GROUP * HEAD_DIM
    nq = t_len // tq
    ctx0 = m_lat // c_len
    return pl.pallas_call(
        _attn_kernel,
        grid=(batch, n_kv, nq),
        in_specs=[pl.BlockSpec((tq, gw), lambda b, h, i: (b * nq + i, h)),
                  pl.BlockSpec((t_len, HEAD_DIM), lambda b, h, i: (b, h)),
                  pl.BlockSpec((t_len, HEAD_DIM), lambda b, h, i: (b, h)),
                  pl.BlockSpec((c_len, HEAD_DIM), lambda b, h, i: (ctx0 + b, h)),
                  pl.BlockSpec((c_len, HEAD_DIM), lambda b, h, i: (ctx0 + b, h)),
                  pl.BlockSpec((tq, gw), lambda b, h, i: (b * nq + i, h))],
        out_specs=pl.BlockSpec((tq, gw), lambda b, h, i: (b * nq + i, h)),
        out_shape=jax.ShapeDtypeStruct((m_lat, aw), BF16),
        compiler_params=_params(("arbitrary", "arbitrary", "arbitrary")),
        name="attn",
    )(q, kk, vv, kk, vv, sg)


def _hgrn_masks(c, reverse):
    ns = c // SUB
    row = lax.broadcasted_iota(jnp.int32, (c, c), 0)
    col = lax.broadcasted_iota(jnp.int32, (c, c), 1)
    pr, pc = row // SUB, col // SUB
    if reverse:
        pr, pc = ns - 1 - pr, ns - 1 - pc
        causal = col >= row
    else:
        causal = col <= row
    m_diag = (pr == pc) & causal
    m_adj = ((pr == 1) & (pc == 0)) | ((pr == 3) & (pc == 2))
    m_far = (pr >= 2) & (pc <= 1)
    return m_diag, m_adj, m_far


def _hgrn_prepare(q, k, bfull, reverse, want_out):
    c = k.shape[0]
    ns = c // SUB
    assert ns == 4
    order = list(range(ns))[::-1] if reverse else list(range(ns))
    zero = jnp.zeros((1, LANES), F32)
    start, end = {}, {}
    for i in range(ns):
        lo, hi = i * SUB, i * SUB + SUB - 1
        if reverse:
            start[i] = bfull[hi + 1:hi + 2, :] if i < ns - 1 else zero
            end[i] = bfull[lo:lo + 1, :]
        else:
            start[i] = bfull[lo - 1:lo, :] if i > 0 else zero
            end[i] = bfull[hi:hi + 1, :]
    btot = end[order[-1]]
    qt, khat, kd, q_far, k_far, qs, ke = {}, {}, {}, {}, {}, {}, {}
    for p, i in enumerate(order):
        rs = slice(i * SUB, (i + 1) * SUB)
        b_i = bfull[rs]
        kd[i] = k[rs] * jnp.exp(end[i] - b_i)
        ke[i] = kd[i] if p == ns - 1 else kd[i] * jnp.exp(btot - end[i])
        if want_out:
            bl = b_i - start[i]
            qt[i] = q[rs] * jnp.exp(bl)
            khat[i] = k[rs] * jnp.exp(-bl)
            qs[i] = qt[i] if p == 0 else qt[i] * jnp.exp(start[i])
            q_far[i] = qt[i] * jnp.exp(start[i] - start[order[2]]) if p == 3 else qt[i]
            k_far[i] = kd[i] * jnp.exp(start[order[2]] - end[i]) if p == 0 else kd[i]

    def cat(d):
        return jnp.concatenate([d[i].astype(BF16) for i in range(ns)], axis=0)

    out = {"ke": cat(ke), "dec": jnp.exp(btot)}
    if want_out:
        out.update(qt=cat(qt), kk=jnp.concatenate([cat(kd), cat(khat)], axis=0),
                   q_far=cat(q_far), k_far=cat(k_far), qs=cat(qs))
    return out


def _hgrn_kernel(*refs, reverse, n_ctx, heads, final):
    if final:
        q_ref, k_ref, g_ref, v_ref, of_ref, sg_ref, hw_ref, o_ref, st_ref = refs
    else:
        q_ref, k_ref, g_ref, v_ref, o_ref, st_ref = refs
    s = pl.program_id(2)

    @pl.when(s == 0)
    def _():
        st_ref[...] = jnp.zeros_like(st_ref)

    c = k_ref.shape[0]
    ii = lax.broadcasted_iota(jnp.int32, (c, c), 0)
    jj = lax.broadcasted_iota(jnp.int32, (c, c), 1)
    tri = jnp.where((jj >= ii) if reverse else (jj <= ii), 1.0, 0.0).astype(BF16)
    g_hi, g_lo = _split_bf16(g_ref[...])
    bfull_all = _dot(tri, g_hi) + _dot(tri, g_lo)

    def run(want_out):
        cols = [slice(j * LANES, (j + 1) * LANES) for j in range(heads)]
        prep = [_hgrn_prepare(q_ref[:, cs].astype(F32) if want_out else None,
                              k_ref[:, cs].astype(F32), bfull_all[:, cs], reverse, want_out)
                for cs in cols]
        if want_out:
            m_diag, m_adj, m_far = _hgrn_masks(c, reverse)
            near = [_dot_nt(p["qt"], p["kk"]) for p in prep]
            far = [_dot_nt(p["q_far"], p["k_far"]) for p in prep]
            inter = [_dot_nt(p["qs"], st_ref[j].astype(BF16)) for j, p in enumerate(prep)]
        for j, (cs, p) in enumerate(zip(cols, prep)):
            st_ref[j] = p["dec"] * st_ref[j] + _dot_tn(v_ref[:, cs], p["ke"])
        if want_out:
            amat = [jnp.where(m_diag, n[:, c:], jnp.where(m_adj, n[:, :c], jnp.where(m_far, f, 0.0))).astype(BF16)
                    for n, f in zip(near, far)]
            for j, cs in enumerate(cols):
                o = _dot(amat[j], v_ref[:, cs]) + inter[j]
                if final:
                    tot = o + of_ref[:, cs]
                    y = tot * lax.rsqrt(jnp.mean(tot * tot, axis=-1, keepdims=True) + EPS)
                    y = y * hw_ref[:, cs]
                    o_ref[:, cs] = (y * sg_ref[:, cs].astype(F32)).astype(o_ref.dtype)
                else:
                    o_ref[:, cs] = o.astype(o_ref.dtype)

    pl.when(s < n_ctx)(lambda: run(False))
    pl.when(s >= n_ctx)(lambda: run(True))


def _hgrn(q, kk, gg, vv, batch, t_len, c_len, reverse, heads, blk, final_args=None):
    m_lat, w = q.shape
    hb = heads * LANES
    n_ctx, n_lat = c_len // blk, t_len // blk
    ctx0 = m_lat // blk

    def row_block(b, s):
        if reverse:
            cblk = ctx0 + b * n_ctx + (n_ctx - 1 - s)
            lblk = b * n_lat + (n_lat - 1 - (s - n_ctx))
        else:
            cblk = ctx0 + b * n_ctx + s
            lblk = b * n_lat + (s - n_ctx)
        return jnp.where(s < n_ctx, cblk, lblk)

    def lat_block(b, s):
        sl = jnp.maximum(s - n_ctx, 0)
        return b * n_lat + ((n_lat - 1 - sl) if reverse else sl)

    all_spec = pl.BlockSpec((blk, hb), lambda b, h, s: (row_block(b, s), h))
    lat_spec = pl.BlockSpec((blk, hb), lambda b, h, s: (lat_block(b, s), h))
    in_specs = [lat_spec, all_spec, all_spec, all_spec]
    args = [q, kk, gg, vv]
    final = final_args is not None
    if final:
        o_f, sg, hw = final_args
        in_specs += [lat_spec, lat_spec, pl.BlockSpec((1, hb), lambda b, h, s: (0, h))]
        args += [o_f, sg, hw]
    return pl.pallas_call(
        functools.partial(_hgrn_kernel, reverse=reverse, n_ctx=n_ctx, heads=heads, final=final),
        grid=(batch, w // hb, n_ctx + n_lat),
        in_specs=in_specs,
        out_specs=lat_spec,
        out_shape=jax.ShapeDtypeStruct((m_lat, w), BF16 if final else F32),
        scratch_shapes=[pltpu.VMEM((heads, LANES, LANES), F32)],
        compiler_params=_params(("arbitrary", "arbitrary", "arbitrary")),
        name="hgrn_bwd" if reverse else "hgrn_fwd",
    )(*args)


def _cast_kernel(w_ref, o_ref):
    o_ref[...] = w_ref[...].astype(BF16)


def _cast_bf16(w, tr):
    r, c = w.shape
    return pl.pallas_call(
        _cast_kernel,
        grid=(r // tr,),
        in_specs=[pl.BlockSpec((tr, c), lambda i: (i, 0))],
        out_specs=pl.BlockSpec((tr, c), lambda i: (i, 0)),
        out_shape=jax.ShapeDtypeStruct((r, c), BF16),
        compiler_params=_params(("arbitrary",)),
        name="cast_bf16",
    )(w)


def _proj_kernel(yh_ref, ya_ref, wh_ref, wa_ref, rh_ref, ra_ref, o_ref):
    a = _dot(yh_ref[...], wh_ref[...])
    b = _dot(ya_ref[...], wa_ref[...])
    o_ref[...] = (rh_ref[...].astype(F32) * a + ra_ref[...].astype(F32) * b).astype(BF16)


def _proj(yh, ya, wh, wa, rh, ra, tm, tn):
    m, k = yh.shape
    n = wh.shape[1]
    return pl.pallas_call(
        _proj_kernel,
        grid=(m // tm, n // tn),
        in_specs=[pl.BlockSpec((tm, k), lambda i, j: (i, 0)),
                  pl.BlockSpec((tm, k), lambda i, j: (i, 0)),
                  pl.BlockSpec((k, tn), lambda i, j: (0, j)),
                  pl.BlockSpec((k, tn), lambda i, j: (0, j)),
                  pl.BlockSpec((tm, tn), lambda i, j: (i, j)),
                  pl.BlockSpec((tm, tn), lambda i, j: (i, j))],
        out_specs=pl.BlockSpec((tm, tn), lambda i, j: (i, j)),
        out_shape=jax.ShapeDtypeStruct((m, n), BF16),
        compiler_params=_params(("arbitrary", "arbitrary")),
        name="proj",
    )(yh, ya, wh, wa, rh, ra)


def _out_kernel(mg_ref, w_ref, x_ref, gate_ref, o_ref):
    o_ref[...] = x_ref[...] + gate_ref[0] * _dot(mg_ref[...], w_ref[...])


def _outproj(mg, w, x2, mod3, t_len, tm, tn):
    m, k = mg.shape
    n = w.shape[1]
    gate0 = 2 * n // tn
    per_b = t_len // tm
    return pl.pallas_call(
        _out_kernel,
        grid=(m // tm, n // tn),
        in_specs=[pl.BlockSpec((tm, k), lambda i, j: (i, 0)),
                  pl.BlockSpec((k, tn), lambda i, j: (0, j)),
                  pl.BlockSpec((tm, tn), lambda i, j: (i, j)),
                  pl.BlockSpec((1, 1, tn), lambda i, j: (i // per_b, 0, gate0 + j))],
        out_specs=pl.BlockSpec((tm, tn), lambda i, j: (i, j)),
        out_shape=jax.ShapeDtypeStruct((m, n), F32),
        compiler_params=_params(("arbitrary", "arbitrary")),
        name="outproj",
    )(mg, w, x2, mod3)


def _fnorm_kernel(x_ref, w_ref, o_ref):
    x = x_ref[...]
    o_ref[...] = x * lax.rsqrt(jnp.mean(x * x, axis=-1, keepdims=True) + EPS) * w_ref[...]


def _fnorm(x2, w, tm):
    m, d = x2.shape
    return pl.pallas_call(
        _fnorm_kernel,
        grid=(m // tm,),
        in_specs=[pl.BlockSpec((tm, d), lambda i: (i, 0)), pl.BlockSpec((1, d), lambda i: (0, 0))],
        out_specs=pl.BlockSpec((tm, d), lambda i: (i, 0)),
        out_shape=jax.ShapeDtypeStruct((m, d), F32),
        compiler_params=_params(("arbitrary",)),
        name="fnorm",
    )(x2, w)


def _rope_tables(t_len, pad_rows):
    axis_dim = HEAD_DIM // 2
    rows = t_len // GRID_W
    row = jnp.repeat(jnp.arange(rows, dtype=F32), GRID_W)
    col = jnp.tile(jnp.arange(GRID_W, dtype=F32), rows)
    inv = ROPE_THETA ** (-jnp.arange(0, axis_dim, 2, dtype=F32) / axis_dim)
    fr = row[:, None] * inv[None]
    fc = col[:, None] * inv[None]
    ang = jnp.concatenate([fr, fr, fc, fc], axis=-1)
    cos, sin = jnp.cos(ang), jnp.sin(ang)
    lane = jnp.arange(HEAD_DIM)
    sign = jnp.where((lane // (HEAD_DIM // 4)) % 2 == 0, -1.0, 1.0).astype(F32)
    cos = jnp.concatenate([cos, jnp.ones((pad_rows, HEAD_DIM), F32)], axis=0)
    sin = jnp.concatenate([sin * sign, jnp.zeros((pad_rows, HEAD_DIM), F32)], axis=0)
    return cos, sin


def _tile(n, pref):
    t = min(n, pref)
    while n % t:
        t //= 2
    return t


def kernel(x, c, ctx, c_ctx, norm_w, w_ada, b_ada, w_in, q_norm_w, k_norm_w, lb_logits,
           hg_norm_w, w_proj_hg, w_proj_att, w_out, final_norm_w):
    batch, t_len, d = x.shape
    c_len = ctx.shape[1]
    depth = norm_w.shape[0]
    assert depth == 1, "single-layer block"
    att_w = d
    kv_w = d // GROUP
    m_lat, m_ctx = batch * t_len, batch * c_len
    m_all = m_lat + m_ctx

    x2 = x.reshape(m_lat, d)
    c2 = ctx.reshape(m_ctx, d)

    pad = (-(batch + 1)) % 8
    s = jnp.concatenate([c, c_ctx[None, :], jnp.zeros((pad, d), F32)], axis=0)
    mod = _ada(s, w_ada[0], b_ada[0][None, :], _tile(3 * d, 512))
    mod3 = mod.reshape(mod.shape[0], 1, 3 * d)

    h = _prenorm(x2, c2, norm_w[0][None, :], mod3, batch, _tile(c_len, 256))

    tm = _tile(m_ctx, 512)
    tn = _tile(kv_w, 512)
    n_lat_tiles = m_lat // tm
    per_seq = t_len // tm
    cos, sin = _rope_tables(t_len, tm)
    rope_specs = [pl.BlockSpec((1, HEAD_DIM), lambda n, m: (0, 0)),
                  pl.BlockSpec((tm, HEAD_DIM), lambda n, m: (jnp.where(m < n_lat_tiles, m % per_seq, per_seq), 0)),
                  pl.BlockSpec((tm, HEAD_DIM), lambda n, m: (jnp.where(m < n_lat_tiles, m % per_seq, per_seq), 0))]
    w = w_in[0]
    col = 0

    def part(width, rows, epilogue, extras=(), extra_specs=(), out_dtypes=(BF16,), name="inproj"):
        nonlocal col
        res = _inproj(h, w, col, width, rows, epilogue, list(extras), list(extra_specs),
                      list(out_dtypes), tm, tn, name)
        col += width
        return res

    (k_att,) = part(kv_w, m_all, functools.partial(_ep_normrope, post_scale=1.0),
                    (k_norm_w[0][None, :], cos, sin), rope_specs, name="inproj_k")
    (v_att,) = part(kv_w, m_all, _ep_plain, name="inproj_v")
    (i_hg,) = part(d, m_all, _ep_plain, name="inproj_i")
    lb_spec = [pl.BlockSpec((lb_logits.shape[1], tn), lambda n, m: (0, n))]
    g_f, k_f = part(d, m_all, _ep_forget, (lb_logits[0],), lb_spec, (F32, BF16), name="inproj_ff")
    g_b, k_b = part(d, m_all, _ep_forget, (lb_logits[1],), lb_spec, (F32, BF16), name="inproj_fb")
    (q_att,) = part(att_w, m_lat, functools.partial(_ep_normrope, post_scale=HEAD_DIM ** -0.5),
                    (q_norm_w[0][None, :], cos, sin), rope_specs, name="inproj_q")
    (sg_att,) = part(att_w, m_lat, _ep_silu, name="inproj_ga")
    (q_hg,) = part(d, m_lat, _ep_silu, name="inproj_qh")
    (sg_hg,) = part(d, m_lat, _ep_silu, name="inproj_gh")
    (sr_hg,) = part(d, m_lat, _ep_sigmoid, name="inproj_rh")
    (sr_att,) = part(d, m_lat, _ep_sigmoid, name="inproj_ra")

    y_att = _attention(q_att, k_att, v_att, sg_att, batch, t_len, c_len, _tile(t_len, 256))

    blk = _tile(c_len, 128)
    heads = min(8, d // LANES)
    o_f = _hgrn(q_hg, k_f, g_f, i_hg, batch, t_len, c_len, False, heads, blk)
    y_hg = _hgrn(q_hg, k_b, g_b, i_hg, batch, t_len, c_len, True, heads, blk,
                 final_args=(o_f, sg_hg, hg_norm_w[0][None, :]))

    tr = _tile(d, 256)
    wh = _cast_bf16(w_proj_hg[0], tr)
    wa = _cast_bf16(w_proj_att[0], tr)
    wo = _cast_bf16(w_out[0], tr)
    tm2 = _tile(t_len, 512)
    tn2 = _tile(d, 512)
    merged = _proj(y_hg, y_att, wh, wa, sr_hg, sr_att, tm2, tn2)
    xn = _outproj(merged, wo, x2, mod3, t_len, tm2, tn2)
    out = _fnorm(xn, final_norm_w[None, :], _tile(m_lat, 256))
    return out.reshape(batch, t_len, d)
```

```python
import functools

import jax
import jax.numpy as jnp
from jax import lax
from jax.experimental import pallas as pl
from jax.experimental.pallas import tpu as pltpu

F32 = jnp.float32
BF16 = jnp.bfloat16

HEAD_DIM = 128
GROUP = 4
GRID_W = 64
ROPE_THETA = 10000.0
EPS = 1e-6
SUB = 32
HG_BLOCK = 4 * SUB
LANES = 128
SUBLANES = 8
MXU_N = 256
LOG2E = 1.4426950408889634
PIECE_ROWS = 256
VMEM_LIMIT = 56 * 1024 * 1024


def _params(sem):
    return pltpu.CompilerParams(dimension_semantics=sem, vmem_limit_bytes=VMEM_LIMIT)


def _sigmoid(x):
    return 1.0 / (1.0 + jnp.exp(-x))


def _dot(a, b):
    return jnp.dot(a, b, preferred_element_type=F32)


def _dot_nt(a, b):
    return lax.dot_general(a, b, (((1,), (1,)), ((), ())), preferred_element_type=F32)


def _dot_tn(a, b):
    return lax.dot_general(a, b, (((0,), (0,)), ((), ())), preferred_element_type=F32)


def _split_bf16(x):
    hi = x.astype(BF16)
    lo = (x - hi.astype(F32)).astype(BF16)
    return hi, lo


def _ada_kernel(s_ref, w_ref, b_ref, o_ref):
    s = s_ref[...]
    s = s * _sigmoid(s)
    s_hi, s_lo = _split_bf16(s)
    w_hi, w_lo = _split_bf16(w_ref[...])
    rows = s.shape[0]
    r = _dot(jnp.concatenate([s_hi, s_lo], axis=0), w_hi)
    o_ref[...] = r[:rows] + r[rows:] + _dot(s_hi, w_lo) + b_ref[...]


def _ada(s, w, b, tn):
    rows, d = s.shape
    n = w.shape[1]
    return pl.pallas_call(
        _ada_kernel,
        grid=(n // tn,),
        in_specs=[pl.BlockSpec((rows, d), lambda j: (0, 0)),
                  pl.BlockSpec((d, tn), lambda j: (0, j)),
                  pl.BlockSpec((1, tn), lambda j: (0, j))],
        out_specs=pl.BlockSpec((rows, tn), lambda j: (0, j)),
        out_shape=jax.ShapeDtypeStruct((rows, n), F32),
        compiler_params=_params(("arbitrary",)),
        name="ada",
    )(s, w, b)


def _prenorm_kernel(x_ref, c_ref, nw_ref, mod_ref, h_ref, *, n_lat, d):
    def emit(src):
        x = src[...]
        y = x * lax.rsqrt(jnp.mean(x * x, axis=-1, keepdims=True) + EPS) * nw_ref[...]
        shift = mod_ref[0, :, 0:d]
        scale = mod_ref[0, :, d:2 * d]
        h_ref[...] = (y * (1.0 + scale) + shift).astype(BF16)

    i = pl.program_id(0)
    pl.when(i < n_lat)(lambda: emit(x_ref))
    pl.when(i >= n_lat)(lambda: emit(c_ref))


def _prenorm(x2, c2, nw, mod3, batch, tm):
    m_lat, d = x2.shape
    m_ctx = c2.shape[0]
    n_lat, n_ctx = m_lat // tm, m_ctx // tm
    per_b = n_lat // batch
    return pl.pallas_call(
        functools.partial(_prenorm_kernel, n_lat=n_lat, d=d),
        grid=(n_lat + n_ctx,),
        in_specs=[pl.BlockSpec((tm, d), lambda i: (jnp.minimum(i, n_lat - 1), 0)),
                  pl.BlockSpec((tm, d), lambda i: (jnp.maximum(i - n_lat, 0), 0)),
                  pl.BlockSpec((1, d), lambda i: (0, 0)),
                  pl.BlockSpec((1, 1, 3 * d), lambda i: (jnp.minimum(i // per_b, batch), 0, 0))],
        out_specs=pl.BlockSpec((tm, d), lambda i: (i, 0)),
        out_shape=jax.ShapeDtypeStruct((m_lat + m_ctx, d), BF16),
        compiler_params=_params(("arbitrary",)),
        name="prenorm",
    )(x2, c2, nw, mod3)


def _ep_plain(acc, rs, cs, n, extras, outs):
    outs[0][rs, cs] = acc.astype(BF16)


def _ep_gate(acc, rs, cs, n, extras, outs, *, n_silu):
    sg = _sigmoid(acc)
    outs[0][rs, cs] = (sg * jnp.where(n < n_silu, acc, 1.0)).astype(BF16)


def _ep_normrope(acc, rs, cs, n, extras, outs, *, post_scale):
    cos_ref, sin_ref, perm_ref = extras
    cos = cos_ref[rs, :]
    sin = sin_ref[rs, :]
    a_hi, a_lo = _split_bf16(acc)
    partner = _dot(jnp.concatenate([a_hi, a_lo], axis=1), perm_ref[...])
    for hh in range(acc.shape[1] // HEAD_DIM):
        hs = slice(hh * HEAD_DIM, (hh + 1) * HEAD_DIM)
        a = acc[:, hs]
        inv = lax.rsqrt(jnp.mean(a * a, axis=-1, keepdims=True) + EPS) * post_scale
        o = (a * cos + partner[:, hs] * sin) * inv
        outs[0][rs, cs.start + hh * HEAD_DIM:cs.start + (hh + 1) * HEAD_DIM] = o.astype(BF16)


def _block_cumsum(g, reverse):
    r, w = g.shape
    sub = lax.broadcasted_iota(jnp.int32, (SUBLANES, w), 0)
    out = []
    for b0 in range(0, r, HG_BLOCK):
        carry, pre = None, []
        for r0 in range(b0, b0 + HG_BLOCK, SUBLANES):
            x = g[r0:r0 + SUBLANES, :]
            for sh in (1, 2, 4):
                x = x + jnp.where(sub >= sh, pltpu.roll(x, sh, 0), 0.0)
            if carry is not None:
                x = x + carry
            carry = x[SUBLANES - 1:SUBLANES, :]
            pre.append(x)
        pre = jnp.concatenate(pre, axis=0)
        out.append(jnp.where(reverse, carry - pre + g[b0:b0 + HG_BLOCK, :], pre))
    return jnp.concatenate(out, axis=0)


def _ep_forget(acc, rs, cs, n, extras, outs, *, n_fwd):
    lg = extras[0][:, cs]
    mx = jnp.max(lg, axis=0, keepdims=True)
    e = jnp.exp(lg - mx)
    lb = e[0:1, :] / jnp.sum(e, axis=0, keepdims=True)
    f = lb + (1.0 - lb) * _sigmoid(acc)
    outs[0][rs, cs] = _block_cumsum(jnp.log(f), n >= n_fwd)
    outs[1][rs, cs] = (1.0 - f).astype(BF16)


def _inproj_kernel(h_ref, w_hbm, *rest, epilogue, n_extra, n_out, col0, tn, chunk, per_step, n_tiles, m_tiles,
                   piece_rows):
    extras = rest[:n_extra]
    outs = rest[n_extra:n_extra + n_out]
    wbf, stage, acc_scr, sem = rest[n_extra + n_out:]
    t = pl.program_id(0)
    steps = n_tiles * m_tiles
    n, m = t // m_tiles, t % m_tiles
    n_prev = jnp.maximum(t - 1, 0) // m_tiles
    k = wbf.shape[1]
    n_chunks = k // chunk

    def chunk_copy(tile, c, s):
        rows = pl.ds(pl.multiple_of(c * chunk, chunk), chunk)
        cols = pl.ds(pl.multiple_of(col0 + tile * tn, LANES), tn)
        return pltpu.make_async_copy(w_hbm.at[rows, cols], stage.at[s], sem.at[s])

    def land(tile, c, s):
        chunk_copy(tile, c, s).wait()
        rows = pl.ds(pl.multiple_of(c * chunk, chunk), chunk)
        wbf[tile % 2, rows, :] = stage[s].astype(BF16)

    @pl.when(t == 0)
    def _():
        acc_scr[...] = jnp.zeros_like(acc_scr)
        chunk_copy(0, 0, 0).start()
        for c in range(n_chunks):
            if c + 1 < n_chunks:
                chunk_copy(0, c + 1, (c + 1) % 2).start()
            land(0, c, c % 2)

    pm = jnp.where(m == 0, m_tiles - 1, m - 1)
    ptile = jnp.where(m == 0, n, n + 1)
    for i in range(per_step):
        c = pm * per_step + i

        @pl.when((t > 0) & (c < n_chunks) & (ptile < n_tiles))
        def _(c=c, i=i):
            land(ptile, c, i)

    for i in range(per_step):
        c = m * per_step + i

        @pl.when((c < n_chunks) & (n + 1 < n_tiles))
        def _(c=c, i=i):
            chunk_copy(n + 1, c, i).start()

    sw = min(MXU_N, tn)
    slices = [slice(j * sw, (j + 1) * sw) for j in range(tn // sw)]
    tm = h_ref.shape[0]
    rp = min(piece_rows, tm)
    rows = [slice(r0, r0 + rp) for r0 in range(0, tm, rp)]

    @pl.when(t < steps)
    def _():
        slot = n % 2
        for cs in slices:
            for rs in rows:
                epilogue(acc_scr[rs, cs], rs, cs, n_prev, extras, outs)
                acc_scr[rs, cs] = _dot(h_ref[rs, :], wbf[slot, :, cs])

    @pl.when(t == steps)
    def _():
        for cs in slices:
            for rs in rows:
                epilogue(acc_scr[rs, cs], rs, cs, n_prev, extras, outs)


def _inproj(h, w, col0, width, rows, epilogue, extras, extra_specs, out_dtypes, tm, tn, name):
    k = h.shape[1]
    n_tiles, m_tiles = width // tn, rows // tm
    steps = n_tiles * m_tiles
    chunk = min(256, k // 2)
    n_chunks = k // chunk
    per_step = -(-n_chunks // m_tiles)
    n_stage = max(2, per_step)

    def prev_tile(t):
        tp = jnp.maximum(t - 1, 0)
        return tp // m_tiles, tp % m_tiles

    def deferred(fn):
        return lambda t: fn(*prev_tile(t))

    in_specs = [pl.BlockSpec((tm, k), lambda t: (jnp.minimum(t, steps - 1) % m_tiles, 0)),
                pl.BlockSpec(memory_space=pl.ANY)]
    in_specs += [pl.BlockSpec(shape, deferred(fn)) for shape, fn in extra_specs]
    out_specs = [pl.BlockSpec((tm, tn), deferred(lambda n, m: (m, n))) for _ in out_dtypes]
    out_shape = [jax.ShapeDtypeStruct((rows, width), dt) for dt in out_dtypes]
    return pl.pallas_call(
        functools.partial(_inproj_kernel, epilogue=epilogue, n_extra=len(extras), n_out=len(out_dtypes),
                          col0=col0, tn=tn, chunk=chunk, per_step=per_step, n_tiles=n_tiles, m_tiles=m_tiles,
                          piece_rows=PIECE_ROWS),
        grid=(steps + 1,),
        in_specs=in_specs,
        out_specs=out_specs,
        out_shape=out_shape,
        scratch_shapes=[pltpu.VMEM((2, k, tn), BF16),
                        pltpu.VMEM((n_stage, chunk, tn), F32),
                        pltpu.VMEM((tm, tn), F32),
                        pltpu.SemaphoreType.DMA((n_stage,))],
        compiler_params=_params(("arbitrary",)),
        name=name,
    )(h, w, *extras)


def _col_reduce(x, op):
    r = x.shape[0]
    while r % (2 * SUBLANES) == 0 and r > SUBLANES:
        r //= 2
        x = op(x[:r], x[r:])
    return x


def _attn_kernel(q_ref, kl_ref, vl_ref, kc_ref, vc_ref, sg_ref, o_ref, k_scr, vt_scr, *, key_chunk):
    t_len = kl_ref.shape[0]
    n_keys = k_scr.shape[0]

    @pl.when(pl.program_id(2) == 0)
    def _():
        k_scr[0:t_len, :] = kl_ref[...]
        k_scr[t_len:, :] = kc_ref[...]
        vt_scr[0:HEAD_DIM, 0:t_len] = vl_ref[...].astype(F32).T.astype(BF16)
        vt_scr[0:HEAD_DIM, t_len:] = vc_ref[...].astype(F32).T.astype(BF16)
        vt_scr[HEAD_DIM:, :] = jnp.ones((vt_scr.shape[0] - HEAD_DIM, vt_scr.shape[1]), BF16)

    heads = [slice(g * HEAD_DIM, (g + 1) * HEAD_DIM) for g in range(GROUP)]
    chunks = [slice(c * key_chunk, (c + 1) * key_chunk) for c in range(n_keys // key_chunk)]

    def scores(g, ks):
        return _dot_nt(k_scr[ks, :], q_ref[:, heads[g]])

    def weigh(s_chunk, mx, ks):
        p = jnp.exp2(s_chunk - mx).astype(BF16)
        return _dot(vt_scr[:, ks], p)

    def col_max(s_chunks):
        part = [_col_reduce(sc, jnp.maximum) for sc in s_chunks]
        return jnp.max(functools.reduce(jnp.maximum, part), axis=0, keepdims=True)

    s_cur = [scores(0, ks) for ks in chunks]
    for g in range(GROUP):
        mx = col_max(s_cur)
        s_nxt, acc = [], None
        for c, ks in enumerate(chunks):
            if g + 1 < GROUP:
                s_nxt.append(scores(g + 1, ks))
            pv = weigh(s_cur[c], mx, ks)
            acc = pv if acc is None else acc + pv
        o = acc[0:HEAD_DIM] / acc[HEAD_DIM:HEAD_DIM + 1]
        o_ref[:, heads[g]] = (o.T * sg_ref[:, heads[g]].astype(F32)).astype(BF16)
        s_cur = s_nxt


def _attention(q, kk, vv, sg, batch, t_len, c_len, tq, v_col0=0):
    m_lat = batch * t_len
    n_kv = kk.shape[1] // HEAD_DIM
    gw = GROUP * HEAD_DIM
    nq = t_len // tq
    ctx0 = m_lat // c_len
    vb = v_col0 // HEAD_DIM
    n_keys = t_len + c_len
    half = n_keys // 2
    key_chunk = half if half % LANES == 0 else n_keys
    return pl.pallas_call(
        functools.partial(_attn_kernel, key_chunk=key_chunk),
        grid=(batch, n_kv, nq),
        in_specs=[pl.BlockSpec((tq, gw), lambda b, h, i: (b * nq + i, h)),
                  pl.BlockSpec((t_len, HEAD_DIM), lambda b, h, i: (b, h)),
                  pl.BlockSpec((t_len, HEAD_DIM), lambda b, h, i: (b, vb + h)),
                  pl.BlockSpec((c_len, HEAD_DIM), lambda b, h, i: (ctx0 + b, h)),
                  pl.BlockSpec((c_len, HEAD_DIM), lambda b, h, i: (ctx0 + b, vb + h)),
                  pl.BlockSpec((tq, gw), lambda b, h, i: (b * nq + i, h))],
        out_specs=pl.BlockSpec((tq, gw), lambda b, h, i: (b * nq + i, h)),
        out_shape=jax.ShapeDtypeStruct((m_lat, n_kv * gw), BF16),
        scratch_shapes=[pltpu.VMEM((t_len + c_len, HEAD_DIM), BF16),
                        pltpu.VMEM((HEAD_DIM + 2 * SUBLANES, t_len + c_len), BF16)],
        compiler_params=_params(("arbitrary", "arbitrary", "arbitrary")),
        name="attn",
    )(q, kk, vv, kk, vv, sg)


def _hgrn_masks(c, reverse):
    ns = c // SUB
    row = lax.broadcasted_iota(jnp.int32, (c, c), 0)
    col = lax.broadcasted_iota(jnp.int32, (c, c), 1)
    pr, pc = row // SUB, col // SUB
    if reverse:
        pr, pc = ns - 1 - pr, ns - 1 - pc
        causal = col >= row
    else:
        causal = col <= row
    m_diag = (pr == pc) & causal
    m_adj = ((pr == 1) & (pc == 0)) | ((pr == 3) & (pc == 2))
    m_far = (pr >= 2) & (pc <= 1)
    return m_diag, m_adj, m_far


def _hgrn_prepare(q, k, bfull, reverse, want_out):
    c = k.shape[0]
    ns = c // SUB
    assert ns == 4
    order = list(range(ns))[::-1] if reverse else list(range(ns))
    zero = jnp.zeros((1, LANES), F32)
    start, end = {}, {}
    for i in range(ns):
        lo, hi = i * SUB, i * SUB + SUB - 1
        if reverse:
            start[i] = bfull[hi + 1:hi + 2, :] if i < ns - 1 else zero
            end[i] = bfull[lo:lo + 1, :]
        else:
            start[i] = bfull[lo - 1:lo, :] if i > 0 else zero
            end[i] = bfull[hi:hi + 1, :]
    btot = end[order[-1]]
    qt, khat, kd, q_far, k_far, qs, ke = {}, {}, {}, {}, {}, {}, {}
    for p, i in enumerate(order):
        rs = slice(i * SUB, (i + 1) * SUB)
        b_i = bfull[rs]
        kd[i] = k[rs] * jnp.exp(end[i] - b_i)
        ke[i] = kd[i] if p == ns - 1 else kd[i] * jnp.exp(btot - end[i])
        if want_out:
            bl = b_i - start[i]
            qt[i] = q[rs] * jnp.exp(bl)
            khat[i] = k[rs] * jnp.exp(-bl)
            qs[i] = qt[i] if p == 0 else qt[i] * jnp.exp(start[i])
            q_far[i] = qt[i] * jnp.exp(start[i] - start[order[2]]) if p == 3 else qt[i]
            k_far[i] = kd[i] * jnp.exp(start[order[2]] - end[i]) if p == 0 else kd[i]

    def cat(d):
        return jnp.concatenate([d[i].astype(BF16) for i in range(ns)], axis=0)

    out = {"ke": cat(ke), "dec": jnp.exp(btot)}
    if want_out:
        out.update(qt=cat(qt), kk=jnp.concatenate([cat(kd), cat(khat)], axis=0),
                   q_far=cat(q_far), k_far=cat(k_far), qs=cat(qs))
    return out


def _hgrn_kernel(*refs, reverse, n_ctx, heads, final):
    if final:
        q_ref, k_ref, b_ref, v_ref, of_ref, sg_ref, hw_ref, o_ref, st_ref = refs
    else:
        q_ref, k_ref, b_ref, v_ref, o_ref, st_ref = refs
    s = pl.program_id(2)

    @pl.when(s == 0)
    def _():
        st_ref[...] = jnp.zeros_like(st_ref)

    c = k_ref.shape[0]

    def run(want_out):
        cols = [slice(j * LANES, (j + 1) * LANES) for j in range(heads)]
        prep = [_hgrn_prepare(q_ref[:, cs].astype(F32) if want_out else None,
                              k_ref[:, cs].astype(F32), b_ref[:, cs], reverse, want_out)
                for cs in cols]
        if want_out:
            m_diag, m_adj, m_far = _hgrn_masks(c, reverse)
            near = [_dot_nt(p["qt"], p["kk"]) for p in prep]
            far = [_dot_nt(p["q_far"], p["k_far"]) for p in prep]
            inter = [_dot_nt(p["qs"], st_ref[j].astype(BF16)) for j, p in enumerate(prep)]
        for j, (cs, p) in enumerate(zip(cols, prep)):
            st_ref[j] = p["dec"] * st_ref[j] + _dot_tn(v_ref[:, cs], p["ke"])
        if want_out:
            amat = [jnp.where(m_diag, n[:, c:], jnp.where(m_adj, n[:, :c], jnp.where(m_far, f, 0.0))).astype(BF16)
                    for n, f in zip(near, far)]
            for j, cs in enumerate(cols):
                o = _dot(amat[j], v_ref[:, cs]) + inter[j]
                if final:
                    tot = o + of_ref[:, cs]
                    y = tot * lax.rsqrt(jnp.mean(tot * tot, axis=-1, keepdims=True) + EPS)
                    y = y * hw_ref[:, cs]
                    o_ref[:, cs] = (y * sg_ref[:, cs].astype(F32)).astype(o_ref.dtype)
                else:
                    o_ref[:, cs] = o.astype(o_ref.dtype)

    pl.when(s < n_ctx)(lambda: run(False))
    pl.when(s >= n_ctx)(lambda: run(True))


def _hgrn(q, kk, bb, vv, w, batch, t_len, c_len, reverse, heads, cols, final_args=None):
    m_lat = batch * t_len
    blk = HG_BLOCK
    hb = heads * LANES
    n_ctx, n_lat = c_len // blk, t_len // blk
    ctx0 = m_lat // blk
    q0, k0, v0, sg0 = (cc // hb for cc in cols)

    def row_block(b, s):
        if reverse:
            cblk = ctx0 + b * n_ctx + (n_ctx - 1 - s)
            lblk = b * n_lat + (n_lat - 1 - (s - n_ctx))
        else:
            cblk = ctx0 + b * n_ctx + s
            lblk = b * n_lat + (s - n_ctx)
        return jnp.where(s < n_ctx, cblk, lblk)

    def lat_block(b, s):
        sl = jnp.maximum(s - n_ctx, 0)
        return b * n_lat + ((n_lat - 1 - sl) if reverse else sl)

    def all_spec(c0):
        return pl.BlockSpec((blk, hb), lambda b, h, s: (row_block(b, s), c0 + h))

    def lat_spec(c0):
        return pl.BlockSpec((blk, hb), lambda b, h, s: (lat_block(b, s), c0 + h))

    in_specs = [lat_spec(q0), all_spec(k0), all_spec(k0), all_spec(v0)]
    args = [q, kk, bb, vv]
    final = final_args is not None
    if final:
        o_f, sg, hw = final_args
        in_specs += [lat_spec(0), lat_spec(sg0), pl.BlockSpec((1, hb), lambda b, h, s: (0, h))]
        args += [o_f, sg, hw]
    return pl.pallas_call(
        functools.partial(_hgrn_kernel, reverse=reverse, n_ctx=n_ctx, heads=heads, final=final),
        grid=(batch, w // hb, n_ctx + n_lat),
        in_specs=in_specs,
        out_specs=lat_spec(0),
        out_shape=jax.ShapeDtypeStruct((m_lat, w), BF16 if final else F32),
        scratch_shapes=[pltpu.VMEM((heads, LANES, LANES), F32)],
        compiler_params=_params(("arbitrary", "arbitrary", "arbitrary")),
        name="hgrn_bwd" if reverse else "hgrn_fwd",
    )(*args)


def _cast_kernel(w_ref, o_ref):
    o_ref[...] = w_ref[...].astype(BF16)


def _cast_bf16(w, tr):
    r, c = w.shape
    return pl.pallas_call(
        _cast_kernel,
        grid=(r // tr,),
        in_specs=[pl.BlockSpec((tr, c), lambda i: (i, 0))],
        out_specs=pl.BlockSpec((tr, c), lambda i: (i, 0)),
        out_shape=jax.ShapeDtypeStruct((r, c), BF16),
        compiler_params=_params(("arbitrary",)),
        name="cast_bf16",
    )(w)


def _proj_kernel(yh_ref, ya_ref, wh_ref, wa_ref, rh_ref, ra_ref, o_ref):
    a = _dot(yh_ref[...], wh_ref[...])
    b = _dot(ya_ref[...], wa_ref[...])
    o_ref[...] = (rh_ref[...].astype(F32) * a + ra_ref[...].astype(F32) * b).astype(BF16)


def _proj(yh, ya, wh, wa, gates, rh_col0, ra_col0, tm, tn):
    m, k = yh.shape
    n = wh.shape[1]
    rh0, ra0 = rh_col0 // tn, ra_col0 // tn
    return pl.pallas_call(
        _proj_kernel,
        grid=(m // tm, n // tn),
        in_specs=[pl.BlockSpec((tm, k), lambda i, j: (i, 0)),
                  pl.BlockSpec((tm, k), lambda i, j: (i, 0)),
                  pl.BlockSpec((k, tn), lambda i, j: (0, j)),
                  pl.BlockSpec((k, tn), lambda i, j: (0, j)),
                  pl.BlockSpec((tm, tn), lambda i, j: (i, rh0 + j)),
                  pl.BlockSpec((tm, tn), lambda i, j: (i, ra0 + j))],
        out_specs=pl.BlockSpec((tm, tn), lambda i, j: (i, j)),
        out_shape=jax.ShapeDtypeStruct((m, n), BF16),
        compiler_params=_params(("arbitrary", "arbitrary")),
        name="proj",
    )(yh, ya, wh, wa, gates, gates)


def _out_kernel(mg_ref, w_ref, x_ref, gate_ref, o_ref):
    o_ref[...] = x_ref[...] + gate_ref[0] * _dot(mg_ref[...], w_ref[...])


def _outproj(mg, w, x2, mod3, t_len, tm, tn):
    m, k = mg.shape
    n = w.shape[1]
    gate0 = 2 * n // tn
    per_b = t_len // tm
    return pl.pallas_call(
        _out_kernel,
        grid=(m // tm, n // tn),
        in_specs=[pl.BlockSpec((tm, k), lambda i, j: (i, 0)),
                  pl.BlockSpec((k, tn), lambda i, j: (0, j)),
                  pl.BlockSpec((tm, tn), lambda i, j: (i, j)),
                  pl.BlockSpec((1, 1, tn), lambda i, j: (i // per_b, 0, gate0 + j))],
        out_specs=pl.BlockSpec((tm, tn), lambda i, j: (i, j)),
        out_shape=jax.ShapeDtypeStruct((m, n), F32),
        compiler_params=_params(("arbitrary", "arbitrary")),
        name="outproj",
    )(mg, w, x2, mod3)


def _fnorm_kernel(x_ref, w_ref, o_ref):
    x = x_ref[...]
    o_ref[...] = x * lax.rsqrt(jnp.mean(x * x, axis=-1, keepdims=True) + EPS) * w_ref[...]


def _fnorm(x2, w, tm):
    m, d = x2.shape
    return pl.pallas_call(
        _fnorm_kernel,
        grid=(m // tm,),
        in_specs=[pl.BlockSpec((tm, d), lambda i: (i, 0)), pl.BlockSpec((1, d), lambda i: (0, 0))],
        out_specs=pl.BlockSpec((tm, d), lambda i: (i, 0)),
        out_shape=jax.ShapeDtypeStruct((m, d), F32),
        compiler_params=_params(("arbitrary",)),
        name="fnorm",
    )(x2, w)


def _rope_tables(t_len, pad_rows):
    axis_dim = HEAD_DIM // 2
    rows = t_len // GRID_W
    row = jnp.repeat(jnp.arange(rows, dtype=F32), GRID_W)
    col = jnp.tile(jnp.arange(GRID_W, dtype=F32), rows)
    inv = ROPE_THETA ** (-jnp.arange(0, axis_dim, 2, dtype=F32) / axis_dim)
    fr = row[:, None] * inv[None]
    fc = col[:, None] * inv[None]
    ang = jnp.concatenate([fr, fr, fc, fc], axis=-1)
    cos, sin = jnp.cos(ang), jnp.sin(ang)
    lane = jnp.arange(HEAD_DIM)
    sign = jnp.where((lane // (HEAD_DIM // 4)) % 2 == 0, -1.0, 1.0).astype(F32)
    cos = jnp.concatenate([cos, jnp.ones((pad_rows, HEAD_DIM), F32)], axis=0)
    sin = jnp.concatenate([sin * sign, jnp.zeros((pad_rows, HEAD_DIM), F32)], axis=0)
    return cos, sin


def _tile(n, pref):
    t = min(n, pref)
    while n % t:
        t //= 2
    return t


def kernel(x, c, ctx, c_ctx, norm_w, w_ada, b_ada, w_in, q_norm_w, k_norm_w, lb_logits,
           hg_norm_w, w_proj_hg, w_proj_att, w_out, final_norm_w):
    batch, t_len, d = x.shape
    c_len = ctx.shape[1]
    depth = norm_w.shape[0]
    assert depth == 1, "single-layer block"
    assert t_len % HG_BLOCK == 0 and c_len % HG_BLOCK == 0
    att_w = d
    kv_w = d // GROUP
    m_lat, m_ctx = batch * t_len, batch * c_len
    m_all = m_lat + m_ctx

    x2 = x.reshape(m_lat, d)
    c2 = ctx.reshape(m_ctx, d)

    pad = (-(batch + 1)) % SUBLANES
    s = jnp.concatenate([c, c_ctx[None, :], jnp.zeros((pad, d), F32)], axis=0)
    mod = _ada(s, w_ada[0], b_ada[0][None, :], _tile(3 * d, 512))
    mod3 = mod.reshape(mod.shape[0], 1, 3 * d)

    h = _prenorm(x2, c2, norm_w[0][None, :], mod3, batch, _tile(c_len, 256))

    tm = _tile(m_ctx, 1024)
    tn = _tile(kv_w, 1024)
    n_lat_tiles = m_lat // tm
    per_seq = t_len // tm
    cos, sin = _rope_tables(t_len, tm)
    sw = min(MXU_N, tn)
    lane = jnp.arange(sw)
    perm = (lane[:, None] == (lane[None, :] ^ (HEAD_DIM // 4))).astype(BF16)
    perm2 = jnp.concatenate([perm, perm], axis=0)
    partner = jnp.arange(HEAD_DIM) ^ (HEAD_DIM // 4)

    def rope_inputs(nw):
        return [cos * nw[None, :], sin * nw[partner][None, :], perm2]

    def rope_row(n, m):
        return (jnp.where(m < n_lat_tiles, m % per_seq, per_seq), 0)

    rope_specs = [((tm, HEAD_DIM), rope_row), ((tm, HEAD_DIM), rope_row), ((2 * sw, sw), lambda n, m: (0, 0))]
    w = w_in[0]
    c_k, c_v, c_f, c_q, c_g = 0, kv_w, 2 * kv_w + d, 2 * kv_w + 3 * d, 2 * kv_w + 3 * d + att_w

    (k_att,) = _inproj(h, w, c_k, kv_w, m_all, functools.partial(_ep_normrope, post_scale=1.0),
                       rope_inputs(k_norm_w[0]), rope_specs, [BF16], tm, tn, "inproj_k")
    (vi,) = _inproj(h, w, c_v, kv_w + d, m_all, _ep_plain, [], [], [BF16], tm, tn, "inproj_vi")
    slots = lb_logits.shape[1]
    lb2 = jnp.transpose(lb_logits, (1, 0, 2)).reshape(slots, 2 * d)
    b_all, k_all = _inproj(h, w, c_f, 2 * d, m_all, functools.partial(_ep_forget, n_fwd=d // tn),
                           [lb2], [((slots, tn), lambda n, m: (0, n))], [F32, BF16], tm, tn, "inproj_f")
    (q_att,) = _inproj(h, w, c_q, att_w, m_lat,
                       functools.partial(_ep_normrope, post_scale=HEAD_DIM ** -0.5 * LOG2E),
                       rope_inputs(q_norm_w[0]), rope_specs, [BF16], tm, tn, "inproj_q")
    (gates,) = _inproj(h, w, c_g, att_w + 4 * d, m_lat,
                       functools.partial(_ep_gate, n_silu=(att_w + 2 * d) // tn),
                       [], [], [BF16], tm, tn, "inproj_g")
    g_qh, g_gh, g_rh, g_ra = att_w, att_w + d, att_w + 2 * d, att_w + 3 * d

    y_att = _attention(q_att, k_att, vi, gates, batch, t_len, c_len, _tile(t_len, 1024))

    heads = _tile(kv_w // LANES, 8)
    o_f = _hgrn(gates, k_all, b_all, vi, d, batch, t_len, c_len, False, heads, (g_qh, 0, kv_w, 0))
    y_hg = _hgrn(gates, k_all, b_all, vi, d, batch, t_len, c_len, True, heads, (g_qh, d, kv_w, g_gh),
                 final_args=(o_f, gates, hg_norm_w[0][None, :]))

    tr = _tile(d, 256)
    wh = _cast_bf16(w_proj_hg[0], tr)
    wa = _cast_bf16(w_proj_att[0], tr)
    wo = _cast_bf16(w_out[0], tr)
    tm2 = _tile(t_len, 512)
    tn2 = _tile(d, 512)
    merged = _proj(y_hg, y_att, wh, wa, gates, g_rh, g_ra, tm2, tn2)
    xn = _outproj(merged, wo, x2, mod3, t_len, tm2, tn2)
    out = _fnorm(xn, final_norm_w[None, :], _tile(m_lat, 256))
    return out.reshape(batch, t_len, d)
```

```python
import functools

import jax
import jax.numpy as jnp
from jax import lax
from jax.experimental import pallas as pl
from jax.experimental.pallas import tpu as pltpu

F32 = jnp.float32
BF16 = jnp.bfloat16

HEAD_DIM = 128
GROUP = 4
GRID_W = 64
ROPE_THETA = 10000.0
EPS = 1e-6
SUB = 32
HG_BLOCK = 4 * SUB
LANES = 128
SUBLANES = 8
MXU_N = 256
LOG2E = 1.4426950408889634
PIECE_ROWS = 256
VMEM_LIMIT = 56 * 1024 * 1024


def _params(sem):
    return pltpu.CompilerParams(dimension_semantics=sem, vmem_limit_bytes=VMEM_LIMIT)


def _sigmoid(x):
    return 1.0 / (1.0 + jnp.exp(-x))


def _dot(a, b):
    return jnp.dot(a, b, preferred_element_type=F32)


def _dot_nt(a, b):
    return lax.dot_general(a, b, (((1,), (1,)), ((), ())), preferred_element_type=F32)


def _dot_tn(a, b):
    return lax.dot_general(a, b, (((0,), (0,)), ((), ())), preferred_element_type=F32)


def _split_bf16(x):
    hi = x.astype(BF16)
    lo = (x - hi.astype(F32)).astype(BF16)
    return hi, lo


def _ada_kernel(s_ref, w_ref, b_ref, o_ref):
    s = s_ref[...]
    s = s * _sigmoid(s)
    s_hi, s_lo = _split_bf16(s)
    w_hi, w_lo = _split_bf16(w_ref[...])
    rows = s.shape[0]
    r = _dot(jnp.concatenate([s_hi, s_lo], axis=0), w_hi)
    o_ref[...] = r[:rows] + r[rows:] + _dot(s_hi, w_lo) + b_ref[...]


def _ada(s, w, b, tn):
    rows, d = s.shape
    n = w.shape[1]
    return pl.pallas_call(
        _ada_kernel,
        grid=(n // tn,),
        in_specs=[pl.BlockSpec((rows, d), lambda j: (0, 0)),
                  pl.BlockSpec((d, tn), lambda j: (0, j)),
                  pl.BlockSpec((1, tn), lambda j: (0, j))],
        out_specs=pl.BlockSpec((rows, tn), lambda j: (0, j)),
        out_shape=jax.ShapeDtypeStruct((rows, n), F32),
        compiler_params=_params(("arbitrary",)),
        name="ada",
    )(s, w, b)


def _prenorm_kernel(x_ref, c_ref, nw_ref, mod_ref, h_ref, *, n_lat, d):
    def emit(src):
        x = src[...]
        y = x * lax.rsqrt(jnp.mean(x * x, axis=-1, keepdims=True) + EPS) * nw_ref[...]
        shift = mod_ref[0, :, 0:d]
        scale = mod_ref[0, :, d:2 * d]
        h_ref[...] = (y * (1.0 + scale) + shift).astype(BF16)

    i = pl.program_id(0)
    pl.when(i < n_lat)(lambda: emit(x_ref))
    pl.when(i >= n_lat)(lambda: emit(c_ref))


def _prenorm(x2, c2, nw, mod3, batch, tm):
    m_lat, d = x2.shape
    m_ctx = c2.shape[0]
    n_lat, n_ctx = m_lat // tm, m_ctx // tm
    per_b = n_lat // batch
    return pl.pallas_call(
        functools.partial(_prenorm_kernel, n_lat=n_lat, d=d),
        grid=(n_lat + n_ctx,),
        in_specs=[pl.BlockSpec((tm, d), lambda i: (jnp.minimum(i, n_lat - 1), 0)),
                  pl.BlockSpec((tm, d), lambda i: (jnp.maximum(i - n_lat, 0), 0)),
                  pl.BlockSpec((1, d), lambda i: (0, 0)),
                  pl.BlockSpec((1, 1, 3 * d), lambda i: (jnp.minimum(i // per_b, batch), 0, 0))],
        out_specs=pl.BlockSpec((tm, d), lambda i: (i, 0)),
        out_shape=jax.ShapeDtypeStruct((m_lat + m_ctx, d), BF16),
        compiler_params=_params(("arbitrary",)),
        name="prenorm",
    )(x2, c2, nw, mod3)


def _ep_plain(acc, rs, cs, n, extras, outs):
    outs[0][rs, cs] = acc.astype(BF16)


def _ep_gate(acc, rs, cs, n, extras, outs, *, n_silu):
    sg = _sigmoid(acc)
    outs[0][rs, cs] = (sg * jnp.where(n < n_silu, acc, 1.0)).astype(BF16)


def _ep_normrope(acc, rs, cs, n, extras, outs, *, post_scale):
    cos_ref, sin_ref, perm_ref = extras
    cos = cos_ref[rs, :]
    sin = sin_ref[rs, :]
    a_hi, a_lo = _split_bf16(acc)
    partner = _dot(jnp.concatenate([a_hi, a_lo], axis=1), perm_ref[...])
    for hh in range(acc.shape[1] // HEAD_DIM):
        hs = slice(hh * HEAD_DIM, (hh + 1) * HEAD_DIM)
        a = acc[:, hs]
        inv = lax.rsqrt(jnp.mean(a * a, axis=-1, keepdims=True) + EPS) * post_scale
        o = (a * cos + partner[:, hs] * sin) * inv
        outs[0][rs, cs.start + hh * HEAD_DIM:cs.start + (hh + 1) * HEAD_DIM] = o.astype(BF16)


def _block_cumsum(g, reverse):
    r, w = g.shape
    sub = lax.broadcasted_iota(jnp.int32, (SUBLANES, w), 0)
    out = []
    for b0 in range(0, r, HG_BLOCK):
        carry, pre = None, []
        for r0 in range(b0, b0 + HG_BLOCK, SUBLANES):
            x = g[r0:r0 + SUBLANES, :]
            for sh in (1, 2, 4):
                x = x + jnp.where(sub >= sh, pltpu.roll(x, sh, 0), 0.0)
            if carry is not None:
                x = x + carry
            carry = x[SUBLANES - 1:SUBLANES, :]
            pre.append(x)
        pre = jnp.concatenate(pre, axis=0)
        out.append(jnp.where(reverse, carry - pre + g[b0:b0 + HG_BLOCK, :], pre))
    return jnp.concatenate(out, axis=0)


def _ep_forget(acc, rs, cs, n, extras, outs, *, n_fwd):
    lg = extras[0][:, cs]
    mx = jnp.max(lg, axis=0, keepdims=True)
    e = jnp.exp(lg - mx)
    lb = e[0:1, :] / jnp.sum(e, axis=0, keepdims=True)
    f = lb + (1.0 - lb) * _sigmoid(acc)
    outs[0][rs, cs] = _block_cumsum(jnp.log(f), n >= n_fwd)
    outs[1][rs, cs] = (1.0 - f).astype(BF16)


def _inproj_kernel(h_ref, w_hbm, *rest, epilogue, n_extra, n_out, col0, tn, chunk, per_step, n_tiles, m_tiles,
                   piece_rows):
    extras = rest[:n_extra]
    outs = rest[n_extra:n_extra + n_out]
    wbf, stage, acc_scr, sem = rest[n_extra + n_out:]
    t = pl.program_id(0)
    steps = n_tiles * m_tiles
    n, m = t // m_tiles, t % m_tiles
    n_prev = jnp.maximum(t - 1, 0) // m_tiles
    k = wbf.shape[1]
    n_chunks = k // chunk

    def chunk_copy(tile, c, s):
        rows = pl.ds(pl.multiple_of(c * chunk, chunk), chunk)
        cols = pl.ds(pl.multiple_of(col0 + tile * tn, LANES), tn)
        return pltpu.make_async_copy(w_hbm.at[rows, cols], stage.at[s], sem.at[s])

    def land(tile, c, s):
        chunk_copy(tile, c, s).wait()
        rows = pl.ds(pl.multiple_of(c * chunk, chunk), chunk)
        wbf[tile % 2, rows, :] = stage[s].astype(BF16)

    @pl.when(t == 0)
    def _():
        acc_scr[...] = jnp.zeros_like(acc_scr)
        chunk_copy(0, 0, 0).start()
        for c in range(n_chunks):
            if c + 1 < n_chunks:
                chunk_copy(0, c + 1, (c + 1) % 2).start()
            land(0, c, c % 2)

    pm = jnp.where(m == 0, m_tiles - 1, m - 1)
    ptile = jnp.where(m == 0, n, n + 1)
    for i in range(per_step):
        c = pm * per_step + i

        @pl.when((t > 0) & (c < n_chunks) & (ptile < n_tiles))
        def _(c=c, i=i):
            land(ptile, c, i)

    for i in range(per_step):
        c = m * per_step + i

        @pl.when((c < n_chunks) & (n + 1 < n_tiles))
        def _(c=c, i=i):
            chunk_copy(n + 1, c, i).start()

    sw = min(MXU_N, tn)
    slices = [slice(j * sw, (j + 1) * sw) for j in range(tn // sw)]
    tm = h_ref.shape[0]
    rp = min(piece_rows, tm)
    rows = [slice(r0, r0 + rp) for r0 in range(0, tm, rp)]

    @pl.when(t < steps)
    def _():
        slot = n % 2
        for cs in slices:
            for rs in rows:
                epilogue(acc_scr[rs, cs], rs, cs, n_prev, extras, outs)
                acc_scr[rs, cs] = _dot(h_ref[rs, :], wbf[slot, :, cs])

    @pl.when(t == steps)
    def _():
        for cs in slices:
            for rs in rows:
                epilogue(acc_scr[rs, cs], rs, cs, n_prev, extras, outs)


def _inproj(h, w, col0, width, rows, epilogue, extras, extra_specs, out_dtypes, tm, tn, name):
    k = h.shape[1]
    n_tiles, m_tiles = width // tn, rows // tm
    steps = n_tiles * m_tiles
    chunk = min(256, k // 2)
    n_chunks = k // chunk
    per_step = -(-n_chunks // m_tiles)
    n_stage = max(2, per_step)

    def prev_tile(t):
        tp = jnp.maximum(t - 1, 0)
        return tp // m_tiles, tp % m_tiles

    def deferred(fn):
        return lambda t: fn(*prev_tile(t))

    in_specs = [pl.BlockSpec((tm, k), lambda t: (jnp.minimum(t, steps - 1) % m_tiles, 0)),
                pl.BlockSpec(memory_space=pl.ANY)]
    in_specs += [pl.BlockSpec(shape, deferred(fn)) for shape, fn in extra_specs]
    out_specs = [pl.BlockSpec((tm, tn), deferred(lambda n, m: (m, n))) for _ in out_dtypes]
    out_shape = [jax.ShapeDtypeStruct((rows, width), dt) for dt in out_dtypes]
    return pl.pallas_call(
        functools.partial(_inproj_kernel, epilogue=epilogue, n_extra=len(extras), n_out=len(out_dtypes),
                          col0=col0, tn=tn, chunk=chunk, per_step=per_step, n_tiles=n_tiles, m_tiles=m_tiles,
                          piece_rows=PIECE_ROWS),
        grid=(steps + 1,),
        in_specs=in_specs,
        out_specs=out_specs,
        out_shape=out_shape,
        scratch_shapes=[pltpu.VMEM((2, k, tn), BF16),
                        pltpu.VMEM((n_stage, chunk, tn), F32),
                        pltpu.VMEM((tm, tn), F32),
                        pltpu.SemaphoreType.DMA((n_stage,))],
        compiler_params=_params(("arbitrary",)),
        name=name,
    )(h, w, *extras)


def _col_reduce(x, op):
    r = x.shape[0]
    while r % (2 * SUBLANES) == 0 and r > SUBLANES:
        r //= 2
        x = op(x[:r], x[r:])
    return x


def _attn_kernel(q_ref, kl_ref, vl_ref, kc_ref, vc_ref, sg_ref, o_ref, k_scr, vt_scr, *, key_chunk):
    t_len = kl_ref.shape[0]
    n_keys = k_scr.shape[0]

    @pl.when(pl.program_id(2) == 0)
    def _():
        k_scr[0:t_len, :] = kl_ref[...]
        k_scr[t_len:, :] = kc_ref[...]
        vt_scr[0:HEAD_DIM, 0:t_len] = vl_ref[...].astype(F32).T.astype(BF16)
        vt_scr[0:HEAD_DIM, t_len:] = vc_ref[...].astype(F32).T.astype(BF16)
        vt_scr[HEAD_DIM:, :] = jnp.ones((vt_scr.shape[0] - HEAD_DIM, vt_scr.shape[1]), BF16)

    heads = [slice(g * HEAD_DIM, (g + 1) * HEAD_DIM) for g in range(GROUP)]
    chunks = [slice(c * key_chunk, (c + 1) * key_chunk) for c in range(n_keys // key_chunk)]

    def scores(g, ks):
        return _dot_nt(k_scr[ks, :], q_ref[:, heads[g]])

    def weigh(s_chunk, mx, ks):
        p = jnp.exp2(s_chunk - mx).astype(BF16)
        return _dot(vt_scr[:, ks], p)

    def col_max(s_chunks):
        part = [_col_reduce(sc, jnp.maximum) for sc in s_chunks]
        return jnp.max(functools.reduce(jnp.maximum, part), axis=0, keepdims=True)

    s_cur = [scores(0, ks) for ks in chunks]
    for g in range(GROUP):
        mx = col_max(s_cur)
        s_nxt, acc = [], None
        for c, ks in enumerate(chunks):
            if g + 1 < GROUP:
                s_nxt.append(scores(g + 1, ks))
            pv = weigh(s_cur[c], mx, ks)
            acc = pv if acc is None else acc + pv
        o = acc[0:HEAD_DIM] / acc[HEAD_DIM:HEAD_DIM + 1]
        o_ref[:, heads[g]] = (o.T * sg_ref[:, heads[g]].astype(F32)).astype(BF16)
        s_cur = s_nxt


def _attention(q, kk, vv, sg, batch, t_len, c_len, tq, v_col0=0):
    m_lat = batch * t_len
    n_kv = kk.shape[1] // HEAD_DIM
    gw = GROUP * HEAD_DIM
    nq = t_len // tq
    ctx0 = m_lat // c_len
    vb = v_col0 // HEAD_DIM
    n_keys = t_len + c_len
    half = n_keys // 2
    key_chunk = half if half % LANES == 0 else n_keys
    return pl.pallas_call(
        functools.partial(_attn_kernel, key_chunk=key_chunk),
        grid=(batch, n_kv, nq),
        in_specs=[pl.BlockSpec((tq, gw), lambda b, h, i: (b * nq + i, h)),
                  pl.BlockSpec((t_len, HEAD_DIM), lambda b, h, i: (b, h)),
                  pl.BlockSpec((t_len, HEAD_DIM), lambda b, h, i: (b, vb + h)),
                  pl.BlockSpec((c_len, HEAD_DIM), lambda b, h, i: (ctx0 + b, h)),
                  pl.BlockSpec((c_len, HEAD_DIM), lambda b, h, i: (ctx0 + b, vb + h)),
                  pl.BlockSpec((tq, gw), lambda b, h, i: (b * nq + i, h))],
        out_specs=pl.BlockSpec((tq, gw), lambda b, h, i: (b * nq + i, h)),
        out_shape=jax.ShapeDtypeStruct((m_lat, n_kv * gw), BF16),
        scratch_shapes=[pltpu.VMEM((t_len + c_len, HEAD_DIM), BF16),
                        pltpu.VMEM((HEAD_DIM + 2 * SUBLANES, t_len + c_len), BF16)],
        compiler_params=_params(("arbitrary", "arbitrary", "arbitrary")),
        name="attn",
    )(q, kk, vv, kk, vv, sg)


def _hgrn_masks(c, reverse):
    ns = c // SUB
    row = lax.broadcasted_iota(jnp.int32, (c, c), 0)
    col = lax.broadcasted_iota(jnp.int32, (c, c), 1)
    pr, pc = row // SUB, col // SUB
    if reverse:
        pr, pc = ns - 1 - pr, ns - 1 - pc
        causal = col >= row
    else:
        causal = col <= row
    m_diag = (pr == pc) & causal
    m_adj = ((pr == 1) & (pc == 0)) | ((pr == 3) & (pc == 2))
    m_far = (pr >= 2) & (pc <= 1)
    return m_diag, m_adj, m_far


def _hgrn_prepare(q, k, bfull, reverse, want_out):
    c = k.shape[0]
    ns = c // SUB
    assert ns == 4
    order = list(range(ns))[::-1] if reverse else list(range(ns))
    zero = jnp.zeros((1, LANES), F32)
    start, end = {}, {}
    for i in range(ns):
        lo, hi = i * SUB, i * SUB + SUB - 1
        if reverse:
            start[i] = bfull[hi + 1:hi + 2, :] if i < ns - 1 else zero
            end[i] = bfull[lo:lo + 1, :]
        else:
            start[i] = bfull[lo - 1:lo, :] if i > 0 else zero
            end[i] = bfull[hi:hi + 1, :]
    btot = end[order[-1]]
    qt, khat, kd, q_far, k_far, qs, ke = {}, {}, {}, {}, {}, {}, {}
    for p, i in enumerate(order):
        rs = slice(i * SUB, (i + 1) * SUB)
        b_i = bfull[rs]
        kd[i] = k[rs] * jnp.exp(end[i] - b_i)
        ke[i] = kd[i] if p == ns - 1 else kd[i] * jnp.exp(btot - end[i])
        if want_out:
            bl = b_i - start[i]
            qt[i] = q[rs] * jnp.exp(bl)
            khat[i] = k[rs] * jnp.exp(-bl)
            qs[i] = qt[i] if p == 0 else qt[i] * jnp.exp(start[i])
            q_far[i] = qt[i] * jnp.exp(start[i] - start[order[2]]) if p == 3 else qt[i]
            k_far[i] = kd[i] * jnp.exp(start[order[2]] - end[i]) if p == 0 else kd[i]

    def cat(d):
        return jnp.concatenate([d[i].astype(BF16) for i in range(ns)], axis=0)

    out = {"ke": cat(ke), "dec": jnp.exp(btot)}
    if want_out:
        out.update(qt=cat(qt), kk=jnp.concatenate([cat(kd), cat(khat)], axis=0),
                   q_far=cat(q_far), k_far=cat(k_far), qs=cat(qs))
    return out


def _hgrn_kernel(*refs, reverse, n_ctx, heads, final):
    if final:
        q_ref, k_ref, b_ref, v_ref, of_ref, sg_ref, hw_ref, o_ref, st_ref = refs
    else:
        q_ref, k_ref, b_ref, v_ref, o_ref, st_ref = refs
    s = pl.program_id(2)

    @pl.when(s == 0)
    def _():
        st_ref[...] = jnp.zeros_like(st_ref)

    c = HG_BLOCK
    n_sub = k_ref.shape[0] // c
    order = list(range(n_sub))[::-1] if reverse else list(range(n_sub))
    cols = [slice(j * LANES, (j + 1) * LANES) for j in range(heads)]

    def run(want_out):
        if want_out:
            m_diag, m_adj, m_far = _hgrn_masks(c, reverse)
        for bi in order:
            rows = slice(bi * c, (bi + 1) * c)
            prep = [_hgrn_prepare(q_ref[rows, cs].astype(F32) if want_out else None,
                                  k_ref[rows, cs].astype(F32), b_ref[rows, cs], reverse, want_out)
                    for cs in cols]
            if want_out:
                near = [_dot_nt(p["qt"], p["kk"]) for p in prep]
                far = [_dot_nt(p["q_far"], p["k_far"]) for p in prep]
                inter = [_dot_nt(p["qs"], st_ref[j].astype(BF16)) for j, p in enumerate(prep)]
            for j, (cs, p) in enumerate(zip(cols, prep)):
                st_ref[j] = p["dec"] * st_ref[j] + _dot_tn(v_ref[rows, cs], p["ke"])
            if want_out:
                amat = [jnp.where(m_diag, n[:, c:], jnp.where(m_adj, n[:, :c], jnp.where(m_far, f, 0.0))).astype(BF16)
                        for n, f in zip(near, far)]
                for j, cs in enumerate(cols):
                    o = _dot(amat[j], v_ref[rows, cs]) + inter[j]
                    if final:
                        tot = o + of_ref[rows, cs]
                        y = tot * lax.rsqrt(jnp.mean(tot * tot, axis=-1, keepdims=True) + EPS)
                        y = y * hw_ref[:, cs]
                        o_ref[rows, cs] = (y * sg_ref[rows, cs].astype(F32)).astype(o_ref.dtype)
                    else:
                        o_ref[rows, cs] = o.astype(o_ref.dtype)

    pl.when(s < n_ctx)(lambda: run(False))
    pl.when(s >= n_ctx)(lambda: run(True))


def _hgrn(q, kk, bb, vv, w, batch, t_len, c_len, reverse, heads, cols, blk, final_args=None):
    m_lat = batch * t_len
    hb = heads * LANES
    n_ctx, n_lat = c_len // blk, t_len // blk
    ctx0 = m_lat // blk
    q0, k0, v0, sg0 = (cc // hb for cc in cols)

    def row_block(b, s):
        if reverse:
            cblk = ctx0 + b * n_ctx + (n_ctx - 1 - s)
            lblk = b * n_lat + (n_lat - 1 - (s - n_ctx))
        else:
            cblk = ctx0 + b * n_ctx + s
            lblk = b * n_lat + (s - n_ctx)
        return jnp.where(s < n_ctx, cblk, lblk)

    def lat_block(b, s):
        sl = jnp.maximum(s - n_ctx, 0)
        return b * n_lat + ((n_lat - 1 - sl) if reverse else sl)

    def all_spec(c0):
        return pl.BlockSpec((blk, hb), lambda b, h, s: (row_block(b, s), c0 + h))

    def lat_spec(c0):
        return pl.BlockSpec((blk, hb), lambda b, h, s: (lat_block(b, s), c0 + h))

    in_specs = [lat_spec(q0), all_spec(k0), all_spec(k0), all_spec(v0)]
    args = [q, kk, bb, vv]
    final = final_args is not None
    if final:
        o_f, sg, hw = final_args
        in_specs += [lat_spec(0), lat_spec(sg0), pl.BlockSpec((1, hb), lambda b, h, s: (0, h))]
        args += [o_f, sg, hw]
    return pl.pallas_call(
        functools.partial(_hgrn_kernel, reverse=reverse, n_ctx=n_ctx, heads=heads, final=final),
        grid=(batch, w // hb, n_ctx + n_lat),
        in_specs=in_specs,
        out_specs=lat_spec(0),
        out_shape=jax.ShapeDtypeStruct((m_lat, w), BF16 if final else F32),
        scratch_shapes=[pltpu.VMEM((heads, LANES, LANES), F32)],
        compiler_params=_params(("arbitrary", "arbitrary", "arbitrary")),
        name="hgrn_bwd" if reverse else "hgrn_fwd",
    )(*args)


def _cast_kernel(w_ref, o_ref):
    o_ref[...] = w_ref[...].astype(BF16)


def _cast_bf16(w, tr):
    r, c = w.shape
    return pl.pallas_call(
        _cast_kernel,
        grid=(r // tr,),
        in_specs=[pl.BlockSpec((tr, c), lambda i: (i, 0))],
        out_specs=pl.BlockSpec((tr, c), lambda i: (i, 0)),
        out_shape=jax.ShapeDtypeStruct((r, c), BF16),
        compiler_params=_params(("arbitrary",)),
        name="cast_bf16",
    )(w)


def _proj_kernel(yh_ref, ya_ref, wh_ref, wa_ref, rh_ref, ra_ref, o_ref):
    a = _dot(yh_ref[...], wh_ref[...])
    b = _dot(ya_ref[...], wa_ref[...])
    o_ref[...] = (rh_ref[...].astype(F32) * a + ra_ref[...].astype(F32) * b).astype(BF16)


def _proj(yh, ya, wh, wa, gates, rh_col0, ra_col0, tm, tn):
    m, k = yh.shape
    n = wh.shape[1]
    rh0, ra0 = rh_col0 // tn, ra_col0 // tn
    return pl.pallas_call(
        _proj_kernel,
        grid=(m // tm, n // tn),
        in_specs=[pl.BlockSpec((tm, k), lambda i, j: (i, 0)),
                  pl.BlockSpec((tm, k), lambda i, j: (i, 0)),
                  pl.BlockSpec((k, tn), lambda i, j: (0, j)),
                  pl.BlockSpec((k, tn), lambda i, j: (0, j)),
                  pl.BlockSpec((tm, tn), lambda i, j: (i, rh0 + j)),
                  pl.BlockSpec((tm, tn), lambda i, j: (i, ra0 + j))],
        out_specs=pl.BlockSpec((tm, tn), lambda i, j: (i, j)),
        out_shape=jax.ShapeDtypeStruct((m, n), BF16),
        compiler_params=_params(("arbitrary", "arbitrary")),
        name="proj",
    )(yh, ya, wh, wa, gates, gates)


def _out_kernel(mg_ref, w_ref, x_ref, gate_ref, fw_ref, o_ref, xn_scr):
    n = pl.program_id(1)
    n_tiles, _, tn = xn_scr.shape
    xn_scr[n] = x_ref[...] + gate_ref[0] * _dot(mg_ref[...], w_ref[...])

    @pl.when(n == n_tiles - 1)
    def _():
        ssq = None
        for j in range(n_tiles):
            v = xn_scr[j]
            part = jnp.sum(v * v, axis=-1, keepdims=True)
            ssq = part if ssq is None else ssq + part
        inv = lax.rsqrt(ssq / (n_tiles * tn) + EPS)
        for j in range(n_tiles):
            cs = slice(j * tn, (j + 1) * tn)
            o_ref[:, cs] = xn_scr[j] * inv * fw_ref[:, cs]


def _outproj(mg, w, x2, mod3, fw, t_len, tm, tn):
    m, k = mg.shape
    n = w.shape[1]
    gate0 = 2 * n // tn
    per_b = t_len // tm
    return pl.pallas_call(
        _out_kernel,
        grid=(m // tm, n // tn),
        in_specs=[pl.BlockSpec((tm, k), lambda i, j: (i, 0)),
                  pl.BlockSpec((k, tn), lambda i, j: (0, j)),
                  pl.BlockSpec((tm, tn), lambda i, j: (i, j)),
                  pl.BlockSpec((1, 1, tn), lambda i, j: (i // per_b, 0, gate0 + j)),
                  pl.BlockSpec((1, n), lambda i, j: (0, 0))],
        out_specs=pl.BlockSpec((tm, n), lambda i, j: (i, 0)),
        out_shape=jax.ShapeDtypeStruct((m, n), F32),
        scratch_shapes=[pltpu.VMEM((n // tn, tm, tn), F32)],
        compiler_params=_params(("arbitrary", "arbitrary")),
        name="outproj",
    )(mg, w, x2, mod3, fw)


def _rope_tables(t_len, pad_rows):
    axis_dim = HEAD_DIM // 2
    rows = t_len // GRID_W
    row = jnp.repeat(jnp.arange(rows, dtype=F32), GRID_W)
    col = jnp.tile(jnp.arange(GRID_W, dtype=F32), rows)
    inv = ROPE_THETA ** (-jnp.arange(0, axis_dim, 2, dtype=F32) / axis_dim)
    fr = row[:, None] * inv[None]
    fc = col[:, None] * inv[None]
    ang = jnp.concatenate([fr, fr, fc, fc], axis=-1)
    cos, sin = jnp.cos(ang), jnp.sin(ang)
    lane = jnp.arange(HEAD_DIM)
    sign = jnp.where((lane // (HEAD_DIM // 4)) % 2 == 0, -1.0, 1.0).astype(F32)
    cos = jnp.concatenate([cos, jnp.ones((pad_rows, HEAD_DIM), F32)], axis=0)
    sin = jnp.concatenate([sin * sign, jnp.zeros((pad_rows, HEAD_DIM), F32)], axis=0)
    return cos, sin


def _tile(n, pref):
    t = min(n, pref)
    while n % t:
        t //= 2
    return t


def kernel(x, c, ctx, c_ctx, norm_w, w_ada, b_ada, w_in, q_norm_w, k_norm_w, lb_logits,
           hg_norm_w, w_proj_hg, w_proj_att, w_out, final_norm_w):
    batch, t_len, d = x.shape
    c_len = ctx.shape[1]
    depth = norm_w.shape[0]
    assert depth == 1, "single-layer block"
    assert t_len % HG_BLOCK == 0 and c_len % HG_BLOCK == 0
    att_w = d
    kv_w = d // GROUP
    m_lat, m_ctx = batch * t_len, batch * c_len
    m_all = m_lat + m_ctx

    x2 = x.reshape(m_lat, d)
    c2 = ctx.reshape(m_ctx, d)

    pad = (-(batch + 1)) % SUBLANES
    s = jnp.concatenate([c, c_ctx[None, :], jnp.zeros((pad, d), F32)], axis=0)
    mod = _ada(s, w_ada[0], b_ada[0][None, :], _tile(3 * d, 512))
    mod3 = mod.reshape(mod.shape[0], 1, 3 * d)

    h = _prenorm(x2, c2, norm_w[0][None, :], mod3, batch, _tile(c_len, 256))

    tm = _tile(m_ctx, 1024)
    tn = _tile(kv_w, 1024)
    n_lat_tiles = m_lat // tm
    per_seq = t_len // tm
    cos, sin = _rope_tables(t_len, tm)
    sw = min(MXU_N, tn)
    lane = jnp.arange(sw)
    perm = (lane[:, None] == (lane[None, :] ^ (HEAD_DIM // 4))).astype(BF16)
    perm2 = jnp.concatenate([perm, perm], axis=0)
    partner = jnp.arange(HEAD_DIM) ^ (HEAD_DIM // 4)

    def rope_inputs(nw):
        return [cos * nw[None, :], sin * nw[partner][None, :], perm2]

    def rope_row(n, m):
        return (jnp.where(m < n_lat_tiles, m % per_seq, per_seq), 0)

    rope_specs = [((tm, HEAD_DIM), rope_row), ((tm, HEAD_DIM), rope_row), ((2 * sw, sw), lambda n, m: (0, 0))]
    w = w_in[0]
    c_k, c_v, c_f, c_q, c_g = 0, kv_w, 2 * kv_w + d, 2 * kv_w + 3 * d, 2 * kv_w + 3 * d + att_w

    (k_att,) = _inproj(h, w, c_k, kv_w, m_all, functools.partial(_ep_normrope, post_scale=1.0),
                       rope_inputs(k_norm_w[0]), rope_specs, [BF16], tm, tn, "inproj_k")
    (vi,) = _inproj(h, w, c_v, kv_w + d, m_all, _ep_plain, [], [], [BF16], tm, tn, "inproj_vi")
    slots = lb_logits.shape[1]
    lb2 = jnp.transpose(lb_logits, (1, 0, 2)).reshape(slots, 2 * d)
    b_all, k_all = _inproj(h, w, c_f, 2 * d, m_all, functools.partial(_ep_forget, n_fwd=d // tn),
                           [lb2], [((slots, tn), lambda n, m: (0, n))], [F32, BF16], tm, tn, "inproj_f")
    (q_att,) = _inproj(h, w, c_q, att_w, m_lat,
                       functools.partial(_ep_normrope, post_scale=HEAD_DIM ** -0.5 * LOG2E),
                       rope_inputs(q_norm_w[0]), rope_specs, [BF16], tm, tn, "inproj_q")
    (gates,) = _inproj(h, w, c_g, att_w + 4 * d, m_lat,
                       functools.partial(_ep_gate, n_silu=(att_w + 2 * d) // tn),
                       [], [], [BF16], tm, tn, "inproj_g")
    g_qh, g_gh, g_rh, g_ra = att_w, att_w + d, att_w + 2 * d, att_w + 3 * d

    y_att = _attention(q_att, k_att, vi, gates, batch, t_len, c_len, _tile(t_len, 1024))

    heads = _tile(kv_w // LANES, 8)
    blk = _tile(c_len, 2 * HG_BLOCK)
    o_f = _hgrn(gates, k_all, b_all, vi, d, batch, t_len, c_len, False, heads, (g_qh, 0, kv_w, 0), blk)
    y_hg = _hgrn(gates, k_all, b_all, vi, d, batch, t_len, c_len, True, heads, (g_qh, d, kv_w, g_gh), blk,
                 final_args=(o_f, gates, hg_norm_w[0][None, :]))

    tr = _tile(d, 256)
    wh = _cast_bf16(w_proj_hg[0], tr)
    wa = _cast_bf16(w_proj_att[0], tr)
    wo = _cast_bf16(w_out[0], tr)
    tm2 = _tile(t_len, 512)
    tn2 = _tile(d, 512)
    merged = _proj(y_hg, y_att, wh, wa, gates, g_rh, g_ra, tm2, tn2)
    out = _outproj(merged, wo, x2, mod3, final_norm_w[None, :], t_len, tm2, tn2)
    return out.reshape(batch, t_len, d)
```

```python
import functools

import jax
import jax.numpy as jnp
from jax import lax
from jax.experimental import pallas as pl
from jax.experimental.pallas import tpu as pltpu

F32 = jnp.float32
BF16 = jnp.bfloat16

HEAD_DIM = 128
GROUP = 4
GRID_W = 64
ROPE_THETA = 10000.0
EPS = 1e-6
SUB = 32
HG_BLOCK = 4 * SUB
LANES = 128
SUBLANES = 8
MXU_N = 256
LOG2E = 1.4426950408889634
PIECE_ROWS = 128
VMEM_LIMIT = 56 * 1024 * 1024


def _params(sem):
    return pltpu.CompilerParams(dimension_semantics=sem, vmem_limit_bytes=VMEM_LIMIT)


def _sigmoid(x):
    return 1.0 / (1.0 + jnp.exp(-x))


def _dot(a, b):
    return jnp.dot(a, b, preferred_element_type=F32)


def _dot_nt(a, b):
    return lax.dot_general(a, b, (((1,), (1,)), ((), ())), preferred_element_type=F32)


def _dot_tn(a, b):
    return lax.dot_general(a, b, (((0,), (0,)), ((), ())), preferred_element_type=F32)


def _split_bf16(x):
    hi = x.astype(BF16)
    lo = (x - hi.astype(F32)).astype(BF16)
    return hi, lo


def _ada_kernel(s_ref, w_ref, b_ref, o_ref):
    s = s_ref[...]
    s = s * _sigmoid(s)
    s_hi, s_lo = _split_bf16(s)
    w_hi, w_lo = _split_bf16(w_ref[...])
    rows = s.shape[0]
    r = _dot(jnp.concatenate([s_hi, s_lo], axis=0), w_hi)
    o_ref[...] = r[:rows] + r[rows:] + _dot(s_hi, w_lo) + b_ref[...]


def _ada(s, w, b, tn):
    rows, d = s.shape
    n = w.shape[1]
    return pl.pallas_call(
        _ada_kernel,
        grid=(n // tn,),
        in_specs=[pl.BlockSpec((rows, d), lambda j: (0, 0)),
                  pl.BlockSpec((d, tn), lambda j: (0, j)),
                  pl.BlockSpec((1, tn), lambda j: (0, j))],
        out_specs=pl.BlockSpec((rows, tn), lambda j: (0, j)),
        out_shape=jax.ShapeDtypeStruct((rows, n), F32),
        compiler_params=_params(("arbitrary",)),
        name="ada",
    )(s, w, b)


def _prenorm_kernel(x_ref, c_ref, nw_ref, mod_ref, h_ref, *, n_lat, d):
    def emit(src):
        x = src[...]
        y = x * lax.rsqrt(jnp.mean(x * x, axis=-1, keepdims=True) + EPS) * nw_ref[...]
        shift = mod_ref[0, :, 0:d]
        scale = mod_ref[0, :, d:2 * d]
        h_ref[...] = (y * (1.0 + scale) + shift).astype(BF16)

    i = pl.program_id(0)
    pl.when(i < n_lat)(lambda: emit(x_ref))
    pl.when(i >= n_lat)(lambda: emit(c_ref))


def _prenorm(x2, c2, nw, mod3, batch, tm):
    m_lat, d = x2.shape
    m_ctx = c2.shape[0]
    n_lat, n_ctx = m_lat // tm, m_ctx // tm
    per_b = n_lat // batch
    return pl.pallas_call(
        functools.partial(_prenorm_kernel, n_lat=n_lat, d=d),
        grid=(n_lat + n_ctx,),
        in_specs=[pl.BlockSpec((tm, d), lambda i: (jnp.minimum(i, n_lat - 1), 0)),
                  pl.BlockSpec((tm, d), lambda i: (jnp.maximum(i - n_lat, 0), 0)),
                  pl.BlockSpec((1, d), lambda i: (0, 0)),
                  pl.BlockSpec((1, 1, 3 * d), lambda i: (jnp.minimum(i // per_b, batch), 0, 0))],
        out_specs=pl.BlockSpec((tm, d), lambda i: (i, 0)),
        out_shape=jax.ShapeDtypeStruct((m_lat + m_ctx, d), BF16),
        compiler_params=_params(("arbitrary",)),
        name="prenorm",
    )(x2, c2, nw, mod3)


def _ep_plain(acc, rs, cs, n, extras, outs):
    outs[0][rs, cs] = acc.astype(BF16)


def _ep_gate(acc, rs, cs, n, extras, outs, *, n_silu):
    sg = _sigmoid(acc)
    outs[0][rs, cs] = (sg * jnp.where(n < n_silu, acc, 1.0)).astype(BF16)


def _ep_normrope(acc, rs, cs, n, extras, outs, *, post_scale):
    cos_ref, sin_ref, perm_ref = extras
    cos = cos_ref[rs, :]
    sin = sin_ref[rs, :]
    a_hi, a_lo = _split_bf16(acc)
    partner = _dot(jnp.concatenate([a_hi, a_lo], axis=1), perm_ref[...])
    for hh in range(acc.shape[1] // HEAD_DIM):
        hs = slice(hh * HEAD_DIM, (hh + 1) * HEAD_DIM)
        a = acc[:, hs]
        inv = lax.rsqrt(jnp.mean(a * a, axis=-1, keepdims=True) + EPS) * post_scale
        o = (a * cos + partner[:, hs] * sin) * inv
        outs[0][rs, cs.start + hh * HEAD_DIM:cs.start + (hh + 1) * HEAD_DIM] = o.astype(BF16)


def _block_cumsum(g, reverse):
    r, w = g.shape
    sub = lax.broadcasted_iota(jnp.int32, (SUBLANES, w), 0)
    out = []
    for b0 in range(0, r, HG_BLOCK):
        carry, pre = None, []
        for r0 in range(b0, b0 + HG_BLOCK, SUBLANES):
            x = g[r0:r0 + SUBLANES, :]
            for sh in (1, 2, 4):
                x = x + jnp.where(sub >= sh, pltpu.roll(x, sh, 0), 0.0)
            if carry is not None:
                x = x + carry
            carry = x[SUBLANES - 1:SUBLANES, :]
            pre.append(x)
        pre = jnp.concatenate(pre, axis=0)
        out.append(jnp.where(reverse, carry - pre + g[b0:b0 + HG_BLOCK, :], pre))
    return jnp.concatenate(out, axis=0)


def _ep_forget(acc, rs, cs, n, extras, outs, *, n_fwd):
    lg = extras[0][:, cs]
    mx = jnp.max(lg, axis=0, keepdims=True)
    e = jnp.exp(lg - mx)
    lb = e[0:1, :] / jnp.sum(e, axis=0, keepdims=True)
    f = lb + (1.0 - lb) * _sigmoid(acc)
    outs[0][rs, cs] = _block_cumsum(jnp.log(f), n >= n_fwd)
    outs[1][rs, cs] = (1.0 - f).astype(BF16)


def _inproj_kernel(h_ref, w_hbm, *rest, epilogue, n_extra, n_out, col0, tn, chunk, per_step, n_tiles, m_tiles,
                   piece_rows):
    extras = rest[:n_extra]
    outs = rest[n_extra:n_extra + n_out]
    wbf, stage, acc_scr, sem = rest[n_extra + n_out:]
    t = pl.program_id(0)
    steps = n_tiles * m_tiles
    n, m = t // m_tiles, t % m_tiles
    n_prev = jnp.maximum(t - 1, 0) // m_tiles
    k = wbf.shape[1]
    n_chunks = k // chunk

    def chunk_copy(tile, c, s):
        rows = pl.ds(pl.multiple_of(c * chunk, chunk), chunk)
        cols = pl.ds(pl.multiple_of(col0 + tile * tn, LANES), tn)
        return pltpu.make_async_copy(w_hbm.at[rows, cols], stage.at[s], sem.at[s])

    def land(tile, c, s):
        chunk_copy(tile, c, s).wait()
        rows = pl.ds(pl.multiple_of(c * chunk, chunk), chunk)
        wbf[tile % 2, rows, :] = stage[s].astype(BF16)

    @pl.when(t == 0)
    def _():
        acc_scr[...] = jnp.zeros_like(acc_scr)
        chunk_copy(0, 0, 0).start()
        for c in range(n_chunks):
            if c + 1 < n_chunks:
                chunk_copy(0, c + 1, (c + 1) % 2).start()
            land(0, c, c % 2)

    pm = jnp.where(m == 0, m_tiles - 1, m - 1)
    ptile = jnp.where(m == 0, n, n + 1)
    for i in range(per_step):
        c = pm * per_step + i

        @pl.when((t > 0) & (c < n_chunks) & (ptile < n_tiles))
        def _(c=c, i=i):
            land(ptile, c, i)

    for i in range(per_step):
        c = m * per_step + i

        @pl.when((c < n_chunks) & (n + 1 < n_tiles))
        def _(c=c, i=i):
            chunk_copy(n + 1, c, i).start()

    sw = min(MXU_N, tn)
    slices = [slice(j * sw, (j + 1) * sw) for j in range(tn // sw)]
    tm = h_ref.shape[0]
    rp = min(piece_rows, tm)
    rows = [slice(r0, r0 + rp) for r0 in range(0, tm, rp)]

    @pl.when(t < steps)
    def _():
        slot = n % 2
        for cs in slices:
            for rs in rows:
                epilogue(acc_scr[rs, cs], rs, cs, n_prev, extras, outs)
                acc_scr[rs, cs] = _dot(h_ref[rs, :], wbf[slot, :, cs])

    @pl.when(t == steps)
    def _():
        for cs in slices:
            for rs in rows:
                epilogue(acc_scr[rs, cs], rs, cs, n_prev, extras, outs)


def _inproj(h, w, col0, width, rows, epilogue, extras, extra_specs, out_dtypes, tm, tn, name):
    k = h.shape[1]
    n_tiles, m_tiles = width // tn, rows // tm
    steps = n_tiles * m_tiles
    chunk = min(256, k // 2)
    n_chunks = k // chunk
    per_step = -(-n_chunks // m_tiles)
    n_stage = max(2, per_step)

    def prev_tile(t):
        tp = jnp.maximum(t - 1, 0)
        return tp // m_tiles, tp % m_tiles

    def deferred(fn):
        return lambda t: fn(*prev_tile(t))

    in_specs = [pl.BlockSpec((tm, k), lambda t: (jnp.minimum(t, steps - 1) % m_tiles, 0)),
                pl.BlockSpec(memory_space=pl.ANY)]
    in_specs += [pl.BlockSpec(shape, deferred(fn)) for shape, fn in extra_specs]
    out_specs = [pl.BlockSpec((tm, tn), deferred(lambda n, m: (m, n))) for _ in out_dtypes]
    out_shape = [jax.ShapeDtypeStruct((rows, width), dt) for dt in out_dtypes]
    return pl.pallas_call(
        functools.partial(_inproj_kernel, epilogue=epilogue, n_extra=len(extras), n_out=len(out_dtypes),
                          col0=col0, tn=tn, chunk=chunk, per_step=per_step, n_tiles=n_tiles, m_tiles=m_tiles,
                          piece_rows=PIECE_ROWS),
        grid=(steps + 1,),
        in_specs=in_specs,
        out_specs=out_specs,
        out_shape=out_shape,
        scratch_shapes=[pltpu.VMEM((2, k, tn), BF16),
                        pltpu.VMEM((n_stage, chunk, tn), F32),
                        pltpu.VMEM((tm, tn), F32),
                        pltpu.SemaphoreType.DMA((n_stage,))],
        compiler_params=_params(("arbitrary",)),
        name=name,
    )(h, w, *extras)


def _col_reduce(x, op):
    r = x.shape[0]
    while r % (2 * SUBLANES) == 0 and r > SUBLANES:
        r //= 2
        x = op(x[:r], x[r:])
    return x


def _attn_kernel(q_ref, kl_ref, vl_ref, kc_ref, vc_ref, sg_ref, *rest, key_chunk, n_side):
    side_in = rest[:n_side]
    o_ref = rest[n_side]
    side_out = rest[n_side + 1:2 * n_side + 1]
    k_scr, vt_scr, st_in, st_out, sem_in, sem_out = rest[2 * n_side + 1:]
    t_len = kl_ref.shape[0]
    n_keys = k_scr.shape[0]
    slab = st_in.shape[1]
    step = (pl.program_id(0) * pl.num_programs(1) + pl.program_id(1)) * pl.num_programs(2) + pl.program_id(2)
    last_step = pl.num_programs(0) * pl.num_programs(1) * pl.num_programs(2) - 1

    def slab_in(w, at_step):
        rows = pl.ds(pl.multiple_of(at_step * slab, slab), slab)
        return pltpu.make_async_copy(side_in[w].at[rows, :], st_in.at[w], sem_in.at[w])

    def slab_out(w, at_step):
        rows = pl.ds(pl.multiple_of(at_step * slab, slab), slab)
        return pltpu.make_async_copy(st_out.at[w], side_out[w].at[rows, :], sem_out.at[w])

    for w in range(n_side):
        slab_in(w, step).start()

    @pl.when(pl.program_id(2) == 0)
    def _():
        k_scr[0:t_len, :] = kl_ref[...]
        k_scr[t_len:, :] = kc_ref[...]
        vt_scr[0:HEAD_DIM, 0:t_len] = vl_ref[...].astype(F32).T.astype(BF16)
        vt_scr[0:HEAD_DIM, t_len:] = vc_ref[...].astype(F32).T.astype(BF16)
        vt_scr[HEAD_DIM:, :] = jnp.ones((vt_scr.shape[0] - HEAD_DIM, vt_scr.shape[1]), BF16)

    heads = [slice(g * HEAD_DIM, (g + 1) * HEAD_DIM) for g in range(GROUP)]
    chunks = [slice(c * key_chunk, (c + 1) * key_chunk) for c in range(n_keys // key_chunk)]

    def scores(g, ks):
        return _dot_nt(k_scr[ks, :], q_ref[:, heads[g]])

    def weigh(s_chunk, mx, ks):
        p = jnp.exp2(s_chunk - mx).astype(BF16)
        return _dot(vt_scr[:, ks], p)

    def col_max(s_chunks):
        part = [_col_reduce(sc, jnp.maximum) for sc in s_chunks]
        return jnp.max(functools.reduce(jnp.maximum, part), axis=0, keepdims=True)

    s_cur = [scores(0, ks) for ks in chunks]
    for g in range(GROUP):
        mx = col_max(s_cur)
        s_nxt, acc = [], None
        for c, ks in enumerate(chunks):
            if g + 1 < GROUP:
                s_nxt.append(scores(g + 1, ks))
            pv = weigh(s_cur[c], mx, ks)
            acc = pv if acc is None else acc + pv
        o = acc[0:HEAD_DIM] / acc[HEAD_DIM:HEAD_DIM + 1]
        o_ref[:, heads[g]] = (o.T * sg_ref[:, heads[g]].astype(F32)).astype(BF16)
        s_cur = s_nxt

    for w in range(n_side):
        slab_in(w, step).wait()

        @pl.when(step > 0)
        def _(w=w):
            slab_out(w, step - 1).wait()

        st_out[w] = st_in[w].astype(BF16)
        slab_out(w, step).start()

        @pl.when(step == last_step)
        def _(w=w):
            slab_out(w, step).wait()


def _attention(q, kk, vv, sg, batch, t_len, c_len, tq, side):
    m_lat = batch * t_len
    n_kv = kk.shape[1] // HEAD_DIM
    gw = GROUP * HEAD_DIM
    nq = t_len // tq
    ctx0 = m_lat // c_len
    n_keys = t_len + c_len
    half = n_keys // 2
    key_chunk = half if half % LANES == 0 else n_keys
    steps = batch * n_kv * nq
    rows, cols = side[0].shape
    assert all(w.shape == (rows, cols) for w in side) and rows % (steps * 2 * SUBLANES) == 0
    slab = rows // steps
    any_spec = pl.BlockSpec(memory_space=pl.ANY)
    return pl.pallas_call(
        functools.partial(_attn_kernel, key_chunk=key_chunk, n_side=len(side)),
        grid=(batch, n_kv, nq),
        in_specs=[pl.BlockSpec((tq, gw), lambda b, h, i: (b * nq + i, h)),
                  pl.BlockSpec((t_len, HEAD_DIM), lambda b, h, i: (b, h)),
                  pl.BlockSpec((t_len, HEAD_DIM), lambda b, h, i: (b, h)),
                  pl.BlockSpec((c_len, HEAD_DIM), lambda b, h, i: (ctx0 + b, h)),
                  pl.BlockSpec((c_len, HEAD_DIM), lambda b, h, i: (ctx0 + b, h)),
                  pl.BlockSpec((tq, gw), lambda b, h, i: (b * nq + i, h))] + [any_spec] * len(side),
        out_specs=[pl.BlockSpec((tq, gw), lambda b, h, i: (b * nq + i, h))] + [any_spec] * len(side),
        out_shape=[jax.ShapeDtypeStruct((m_lat, n_kv * gw), BF16)]
                  + [jax.ShapeDtypeStruct((rows, cols), BF16) for _ in side],
        scratch_shapes=[pltpu.VMEM((n_keys, HEAD_DIM), BF16),
                        pltpu.VMEM((HEAD_DIM + 2 * SUBLANES, n_keys), BF16),
                        pltpu.VMEM((len(side), slab, cols), F32),
                        pltpu.VMEM((len(side), slab, cols), BF16),
                        pltpu.SemaphoreType.DMA((len(side),)),
                        pltpu.SemaphoreType.DMA((len(side),))],
        compiler_params=_params(("arbitrary", "arbitrary", "arbitrary")),
        name="attn",
    )(q, kk, vv, kk, vv, sg, *side)


def _hgrn_masks(c, reverse):
    ns = c // SUB
    row = lax.broadcasted_iota(jnp.int32, (c, c), 0)
    col = lax.broadcasted_iota(jnp.int32, (c, c), 1)
    pr, pc = row // SUB, col // SUB
    if reverse:
        pr, pc = ns - 1 - pr, ns - 1 - pc
        causal = col >= row
    else:
        causal = col <= row
    m_diag = (pr == pc) & causal
    m_adj = ((pr == 1) & (pc == 0)) | ((pr == 3) & (pc == 2))
    m_far = (pr >= 2) & (pc <= 1)
    return m_diag, m_adj, m_far


def _hgrn_prepare(q, k, bfull, reverse, want_out):
    c = k.shape[0]
    ns = c // SUB
    assert ns == 4
    order = list(range(ns))[::-1] if reverse else list(range(ns))
    zero = jnp.zeros((1, LANES), F32)
    start, end = {}, {}
    for i in range(ns):
        lo, hi = i * SUB, i * SUB + SUB - 1
        if reverse:
            start[i] = bfull[hi + 1:hi + 2, :] if i < ns - 1 else zero
            end[i] = bfull[lo:lo + 1, :]
        else:
            start[i] = bfull[lo - 1:lo, :] if i > 0 else zero
            end[i] = bfull[hi:hi + 1, :]
    btot = end[order[-1]]
    qt, khat, kd, q_far, k_far, qs, ke = {}, {}, {}, {}, {}, {}, {}
    for p, i in enumerate(order):
        rs = slice(i * SUB, (i + 1) * SUB)
        b_i = bfull[rs]
        kd[i] = k[rs] * jnp.exp(end[i] - b_i)
        ke[i] = kd[i] if p == ns - 1 else kd[i] * jnp.exp(btot - end[i])
        if want_out:
            bl = b_i - start[i]
            qt[i] = q[rs] * jnp.exp(bl)
            khat[i] = k[rs] * jnp.exp(-bl)
            qs[i] = qt[i] if p == 0 else qt[i] * jnp.exp(start[i])
            q_far[i] = qt[i] * jnp.exp(start[i] - start[order[2]]) if p == 3 else qt[i]
            k_far[i] = kd[i] * jnp.exp(start[order[2]] - end[i]) if p == 0 else kd[i]

    def cat(d):
        return jnp.concatenate([d[i].astype(BF16) for i in range(ns)], axis=0)

    out = {"ke": cat(ke), "dec": jnp.exp(btot)}
    if want_out:
        out.update(qt=cat(qt), kk=jnp.concatenate([cat(kd), cat(khat)], axis=0),
                   q_far=cat(q_far), k_far=cat(k_far), qs=cat(qs))
    return out


def _hgrn_kernel(*refs, reverse, n_ctx, heads, final):
    if final:
        q_ref, k_ref, b_ref, v_ref, of_ref, sg_ref, hw_ref, o_ref, st_ref = refs
    else:
        q_ref, k_ref, b_ref, v_ref, o_ref, st_ref = refs
    s = pl.program_id(2)

    @pl.when(s == 0)
    def _():
        st_ref[...] = jnp.zeros_like(st_ref)

    c = HG_BLOCK
    n_sub = k_ref.shape[0] // c
    order = list(range(n_sub))[::-1] if reverse else list(range(n_sub))
    cols = [slice(j * LANES, (j + 1) * LANES) for j in range(heads)]

    def run(want_out):
        if want_out:
            m_diag, m_adj, m_far = _hgrn_masks(c, reverse)
        for bi in order:
            rows = slice(bi * c, (bi + 1) * c)
            prep = [_hgrn_prepare(q_ref[rows, cs].astype(F32) if want_out else None,
                                  k_ref[rows, cs].astype(F32), b_ref[rows, cs], reverse, want_out)
                    for cs in cols]
            if want_out:
                near = [_dot_nt(p["qt"], p["kk"]) for p in prep]
                far = [_dot_nt(p["q_far"], p["k_far"]) for p in prep]
                inter = [_dot_nt(p["qs"], st_ref[j].astype(BF16)) for j, p in enumerate(prep)]
            for j, (cs, p) in enumerate(zip(cols, prep)):
                st_ref[j] = p["dec"] * st_ref[j] + _dot_tn(v_ref[rows, cs], p["ke"])
            if want_out:
                amat = [jnp.where(m_diag, n[:, c:], jnp.where(m_adj, n[:, :c], jnp.where(m_far, f, 0.0))).astype(BF16)
                        for n, f in zip(near, far)]
                for j, cs in enumerate(cols):
                    o = _dot(amat[j], v_ref[rows, cs]) + inter[j]
                    if final:
                        tot = o + of_ref[rows, cs]
                        y = tot * lax.rsqrt(jnp.mean(tot * tot, axis=-1, keepdims=True) + EPS)
                        y = y * hw_ref[:, cs]
                        o_ref[rows, cs] = (y * sg_ref[rows, cs].astype(F32)).astype(o_ref.dtype)
                    else:
                        o_ref[rows, cs] = o.astype(o_ref.dtype)

    pl.when(s < n_ctx)(lambda: run(False))
    pl.when(s >= n_ctx)(lambda: run(True))


def _hgrn(q, kk, bb, vv, w, batch, t_len, c_len, reverse, heads, cols, blk, final_args=None):
    m_lat = batch * t_len
    hb = heads * LANES
    n_ctx, n_lat = c_len // blk, t_len // blk
    ctx0 = m_lat // blk
    q0, k0, v0, sg0 = (cc // hb for cc in cols)

    def row_block(b, s):
        if reverse:
            cblk = ctx0 + b * n_ctx + (n_ctx - 1 - s)
            lblk = b * n_lat + (n_lat - 1 - (s - n_ctx))
        else:
            cblk = ctx0 + b * n_ctx + s
            lblk = b * n_lat + (s - n_ctx)
        return jnp.where(s < n_ctx, cblk, lblk)

    def lat_block(b, s):
        sl = jnp.maximum(s - n_ctx, 0)
        return b * n_lat + ((n_lat - 1 - sl) if reverse else sl)

    def all_spec(c0):
        return pl.BlockSpec((blk, hb), lambda b, h, s: (row_block(b, s), c0 + h))

    def lat_spec(c0):
        return pl.BlockSpec((blk, hb), lambda b, h, s: (lat_block(b, s), c0 + h))

    in_specs = [lat_spec(q0), all_spec(k0), all_spec(k0), all_spec(v0)]
    args = [q, kk, bb, vv]
    final = final_args is not None
    if final:
        o_f, sg, hw = final_args
        in_specs += [lat_spec(0), lat_spec(sg0), pl.BlockSpec((1, hb), lambda b, h, s: (0, h))]
        args += [o_f, sg, hw]
    return pl.pallas_call(
        functools.partial(_hgrn_kernel, reverse=reverse, n_ctx=n_ctx, heads=heads, final=final),
        grid=(batch, w // hb, n_ctx + n_lat),
        in_specs=in_specs,
        out_specs=lat_spec(0),
        out_shape=jax.ShapeDtypeStruct((m_lat, w), BF16 if final else F32),
        scratch_shapes=[pltpu.VMEM((heads, LANES, LANES), F32)],
        compiler_params=_params(("arbitrary", "arbitrary", "arbitrary")),
        name="hgrn_bwd" if reverse else "hgrn_fwd",
    )(*args)


def _proj_kernel(yh_ref, ya_ref, wh_ref, wa_ref, rh_ref, ra_ref, o_ref):
    a = _dot(yh_ref[...], wh_ref[...])
    b = _dot(ya_ref[...], wa_ref[...])
    o_ref[...] = (rh_ref[...].astype(F32) * a + ra_ref[...].astype(F32) * b).astype(BF16)


def _proj(yh, ya, wh, wa, gates, rh_col0, ra_col0, tm, tn):
    m, k = yh.shape
    n = wh.shape[1]
    rh0, ra0 = rh_col0 // tn, ra_col0 // tn
    return pl.pallas_call(
        _proj_kernel,
        grid=(m // tm, n // tn),
        in_specs=[pl.BlockSpec((tm, k), lambda i, j: (i, 0)),
                  pl.BlockSpec((tm, k), lambda i, j: (i, 0)),
                  pl.BlockSpec((k, tn), lambda i, j: (0, j)),
                  pl.BlockSpec((k, tn), lambda i, j: (0, j)),
                  pl.BlockSpec((tm, tn), lambda i, j: (i, rh0 + j)),
                  pl.BlockSpec((tm, tn), lambda i, j: (i, ra0 + j))],
        out_specs=pl.BlockSpec((tm, tn), lambda i, j: (i, j)),
        out_shape=jax.ShapeDtypeStruct((m, n), BF16),
        compiler_params=_params(("arbitrary", "arbitrary")),
        name="proj",
    )(yh, ya, wh, wa, gates, gates)


def _out_kernel(mg_ref, w_ref, x_ref, gate_ref, fw_ref, o_ref, xn_scr):
    n = pl.program_id(1)
    n_tiles, _, tn = xn_scr.shape
    xn_scr[n] = x_ref[...] + gate_ref[0] * _dot(mg_ref[...], w_ref[...])

    @pl.when(n == n_tiles - 1)
    def _():
        ssq = None
        for j in range(n_tiles):
            v = xn_scr[j]
            part = jnp.sum(v * v, axis=-1, keepdims=True)
            ssq = part if ssq is None else ssq + part
        inv = lax.rsqrt(ssq / (n_tiles * tn) + EPS)
        for j in range(n_tiles):
            cs = slice(j * tn, (j + 1) * tn)
            o_ref[:, cs] = xn_scr[j] * inv * fw_ref[:, cs]


def _outproj(mg, w, x2, mod3, fw, t_len, tm, tn):
    m, k = mg.shape
    n = w.shape[1]
    gate0 = 2 * n // tn
    per_b = t_len // tm
    return pl.pallas_call(
        _out_kernel,
        grid=(m // tm, n // tn),
        in_specs=[pl.BlockSpec((tm, k), lambda i, j: (i, 0)),
                  pl.BlockSpec((k, tn), lambda i, j: (0, j)),
                  pl.BlockSpec((tm, tn), lambda i, j: (i, j)),
                  pl.BlockSpec((1, 1, tn), lambda i, j: (i // per_b, 0, gate0 + j)),
                  pl.BlockSpec((1, n), lambda i, j: (0, 0))],
        out_specs=pl.BlockSpec((tm, n), lambda i, j: (i, 0)),
        out_shape=jax.ShapeDtypeStruct((m, n), F32),
        scratch_shapes=[pltpu.VMEM((n // tn, tm, tn), F32)],
        compiler_params=_params(("arbitrary", "arbitrary")),
        name="outproj",
    )(mg, w, x2, mod3, fw)


def _rope_tables(t_len, pad_rows):
    axis_dim = HEAD_DIM // 2
    rows = t_len // GRID_W
    row = jnp.repeat(jnp.arange(rows, dtype=F32), GRID_W)
    col = jnp.tile(jnp.arange(GRID_W, dtype=F32), rows)
    inv = ROPE_THETA ** (-jnp.arange(0, axis_dim, 2, dtype=F32) / axis_dim)
    fr = row[:, None] * inv[None]
    fc = col[:, None] * inv[None]
    ang = jnp.concatenate([fr, fr, fc, fc], axis=-1)
    cos, sin = jnp.cos(ang), jnp.sin(ang)
    lane = jnp.arange(HEAD_DIM)
    sign = jnp.where((lane // (HEAD_DIM // 4)) % 2 == 0, -1.0, 1.0).astype(F32)
    cos = jnp.concatenate([cos, jnp.ones((pad_rows, HEAD_DIM), F32)], axis=0)
    sin = jnp.concatenate([sin * sign, jnp.zeros((pad_rows, HEAD_DIM), F32)], axis=0)
    return cos, sin


def _tile(n, pref):
    t = min(n, pref)
    while n % t:
        t //= 2
    return t


def kernel(x, c, ctx, c_ctx, norm_w, w_ada, b_ada, w_in, q_norm_w, k_norm_w, lb_logits,
           hg_norm_w, w_proj_hg, w_proj_att, w_out, final_norm_w):
    batch, t_len, d = x.shape
    c_len = ctx.shape[1]
    depth = norm_w.shape[0]
    assert depth == 1, "single-layer block"
    assert t_len % HG_BLOCK == 0 and c_len % HG_BLOCK == 0
    att_w = d
    kv_w = d // GROUP
    m_lat, m_ctx = batch * t_len, batch * c_len
    m_all = m_lat + m_ctx

    x2 = x.reshape(m_lat, d)
    c2 = ctx.reshape(m_ctx, d)

    pad = (-(batch + 1)) % SUBLANES
    s = jnp.concatenate([c, c_ctx[None, :], jnp.zeros((pad, d), F32)], axis=0)
    mod = _ada(s, w_ada[0], b_ada[0][None, :], _tile(3 * d, 512))
    mod3 = mod.reshape(mod.shape[0], 1, 3 * d)

    h = _prenorm(x2, c2, norm_w[0][None, :], mod3, batch, _tile(c_len, 256))

    tm = _tile(m_ctx, 1024)
    tn = _tile(kv_w, 1024)
    n_lat_tiles = m_lat // tm
    per_seq = t_len // tm
    cos, sin = _rope_tables(t_len, tm)
    sw = min(MXU_N, tn)
    lane = jnp.arange(sw)
    perm = (lane[:, None] == (lane[None, :] ^ (HEAD_DIM // 4))).astype(BF16)
    perm2 = jnp.concatenate([perm, perm], axis=0)
    partner = jnp.arange(HEAD_DIM) ^ (HEAD_DIM // 4)

    def rope_inputs(nw):
        return [cos * nw[None, :], sin * nw[partner][None, :], perm2]

    def rope_row(n, m):
        return (jnp.where(m < n_lat_tiles, m % per_seq, per_seq), 0)

    rope_specs = [((tm, HEAD_DIM), rope_row), ((tm, HEAD_DIM), rope_row), ((2 * sw, sw), lambda n, m: (0, 0))]
    w = w_in[0]
    c_k, c_v, c_f, c_q, c_g = 0, kv_w, 2 * kv_w + d, 2 * kv_w + 3 * d, 2 * kv_w + 3 * d + att_w

    (k_att,) = _inproj(h, w, c_k, kv_w, m_all, functools.partial(_ep_normrope, post_scale=1.0),
                       rope_inputs(k_norm_w[0]), rope_specs, [BF16], tm, tn, "inproj_k")
    (vi,) = _inproj(h, w, c_v, kv_w + d, m_all, _ep_plain, [], [], [BF16], tm, tn, "inproj_vi")
    slots = lb_logits.shape[1]
    lb2 = jnp.transpose(lb_logits, (1, 0, 2)).reshape(slots, 2 * d)
    b_all, k_all = _inproj(h, w, c_f, 2 * d, m_all, functools.partial(_ep_forget, n_fwd=d // tn),
                           [lb2], [((slots, tn), lambda n, m: (0, n))], [F32, BF16], tm, tn, "inproj_f")
    (q_att,) = _inproj(h, w, c_q, att_w, m_lat,
                       functools.partial(_ep_normrope, post_scale=HEAD_DIM ** -0.5 * LOG2E),
                       rope_inputs(q_norm_w[0]), rope_specs, [BF16], tm, tn, "inproj_q")
    (gates,) = _inproj(h, w, c_g, att_w + 4 * d, m_lat,
                       functools.partial(_ep_gate, n_silu=(att_w + 2 * d) // tn),
                       [], [], [BF16], tm, tn, "inproj_g")
    g_qh, g_gh, g_rh, g_ra = att_w, att_w + d, att_w + 2 * d, att_w + 3 * d

    y_att, wh, wa, wo = _attention(q_att, k_att, vi, gates, batch, t_len, c_len, _tile(t_len, 1024),
                                   [w_proj_hg[0], w_proj_att[0], w_out[0]])

    heads = _tile(kv_w // LANES, 8)
    blk = _tile(c_len, 2 * HG_BLOCK)
    o_f = _hgrn(gates, k_all, b_all, vi, d, batch, t_len, c_len, False, heads, (g_qh, 0, kv_w, 0), blk)
    y_hg = _hgrn(gates, k_all, b_all, vi, d, batch, t_len, c_len, True, heads, (g_qh, d, kv_w, g_gh), blk,
                 final_args=(o_f, gates, hg_norm_w[0][None, :]))

    tm2 = _tile(t_len, 512)
    tn2 = _tile(d, 512)
    merged = _proj(y_hg, y_att, wh, wa, gates, g_rh, g_ra, tm2, tn2)
    out = _outproj(merged, wo, x2, mod3, final_norm_w[None, :], t_len, tm2, tn2)
    return out.reshape(batch, t_len, d)
```

```python
import functools

import jax
import jax.numpy as jnp
from jax import lax
from jax.experimental import pallas as pl
from jax.experimental.pallas import tpu as pltpu

F32 = jnp.float32
BF16 = jnp.bfloat16

HEAD_DIM = 128
GROUP = 4
GRID_W = 64
ROPE_THETA = 10000.0
EPS = 1e-6
SUB = 32
HG_BLOCK = 4 * SUB
LANES = 128
SUBLANES = 8
MXU_N = 256
LOG2E = 1.4426950408889634
PIECE_ROWS = 256
VMEM_LIMIT = 56 * 1024 * 1024


def _params(sem):
    return pltpu.CompilerParams(dimension_semantics=sem, vmem_limit_bytes=VMEM_LIMIT)


def _sigmoid(x):
    return 1.0 / (1.0 + jnp.exp(-x))


def _dot(a, b):
    return jnp.dot(a, b, preferred_element_type=F32)


def _dot_nt(a, b):
    return lax.dot_general(a, b, (((1,), (1,)), ((), ())), preferred_element_type=F32)


def _dot_tn(a, b):
    return lax.dot_general(a, b, (((0,), (0,)), ((), ())), preferred_element_type=F32)


def _split_bf16(x):
    hi = x.astype(BF16)
    lo = (x - hi.astype(F32)).astype(BF16)
    return hi, lo


def _ada_kernel(s_ref, w_ref, b_ref, o_ref):
    s = s_ref[...]
    s = s * _sigmoid(s)
    s_hi, s_lo = _split_bf16(s)
    w_hi, w_lo = _split_bf16(w_ref[...])
    rows = s.shape[0]
    r = _dot(jnp.concatenate([s_hi, s_lo], axis=0), w_hi)
    o_ref[...] = r[:rows] + r[rows:] + _dot(s_hi, w_lo) + b_ref[...]


def _ada(s, w, b, tn):
    rows, d = s.shape
    n = w.shape[1]
    return pl.pallas_call(
        _ada_kernel,
        grid=(n // tn,),
        in_specs=[pl.BlockSpec((rows, d), lambda j: (0, 0)),
                  pl.BlockSpec((d, tn), lambda j: (0, j)),
                  pl.BlockSpec((1, tn), lambda j: (0, j))],
        out_specs=pl.BlockSpec((rows, tn), lambda j: (0, j)),
        out_shape=jax.ShapeDtypeStruct((rows, n), F32),
        compiler_params=_params(("arbitrary",)),
        name="ada",
    )(s, w, b)


def _prenorm_kernel(x_ref, c_ref, nw_ref, mod_ref, h_ref, *, n_lat, d):
    def emit(src):
        x = src[...]
        y = x * lax.rsqrt(jnp.mean(x * x, axis=-1, keepdims=True) + EPS) * nw_ref[...]
        shift = mod_ref[0, :, 0:d]
        scale = mod_ref[0, :, d:2 * d]
        h_ref[...] = (y * (1.0 + scale) + shift).astype(BF16)

    i = pl.program_id(0)
    pl.when(i < n_lat)(lambda: emit(x_ref))
    pl.when(i >= n_lat)(lambda: emit(c_ref))


def _prenorm(x2, c2, nw, mod3, batch, tm):
    m_lat, d = x2.shape
    m_ctx = c2.shape[0]
    n_lat, n_ctx = m_lat // tm, m_ctx // tm
    per_b = n_lat // batch
    return pl.pallas_call(
        functools.partial(_prenorm_kernel, n_lat=n_lat, d=d),
        grid=(n_lat + n_ctx,),
        in_specs=[pl.BlockSpec((tm, d), lambda i: (jnp.minimum(i, n_lat - 1), 0)),
                  pl.BlockSpec((tm, d), lambda i: (jnp.maximum(i - n_lat, 0), 0)),
                  pl.BlockSpec((1, d), lambda i: (0, 0)),
                  pl.BlockSpec((1, 1, 3 * d), lambda i: (jnp.minimum(i // per_b, batch), 0, 0))],
        out_specs=pl.BlockSpec((tm, d), lambda i: (i, 0)),
        out_shape=jax.ShapeDtypeStruct((m_lat + m_ctx, d), BF16),
        compiler_params=_params(("arbitrary",)),
        name="prenorm",
    )(x2, c2, nw, mod3)


def _ep_plain(acc, rs, cs, n, extras, outs):
    outs[0][rs, cs] = acc.astype(BF16)


def _ep_gate(acc, rs, cs, n, extras, outs, *, n_silu):
    sg = _sigmoid(acc)
    outs[0][rs, cs] = (sg * jnp.where(n < n_silu, acc, 1.0)).astype(BF16)


def _ep_normrope(acc, rs, cs, n, extras, outs, *, post_scale):
    cos_ref, sin_ref, perm_ref = extras
    cos = cos_ref[rs, :]
    sin = sin_ref[rs, :]
    a_hi, a_lo = _split_bf16(acc)
    partner = _dot(jnp.concatenate([a_hi, a_lo], axis=1), perm_ref[...])
    for hh in range(acc.shape[1] // HEAD_DIM):
        hs = slice(hh * HEAD_DIM, (hh + 1) * HEAD_DIM)
        a = acc[:, hs]
        inv = lax.rsqrt(jnp.mean(a * a, axis=-1, keepdims=True) + EPS) * post_scale
        o = (a * cos + partner[:, hs] * sin) * inv
        outs[0][rs, cs.start + hh * HEAD_DIM:cs.start + (hh + 1) * HEAD_DIM] = o.astype(BF16)


def _block_cumsum(g, reverse):
    r, w = g.shape
    sub = lax.broadcasted_iota(jnp.int32, (SUBLANES, w), 0)
    out = []
    for b0 in range(0, r, HG_BLOCK):
        carry, pre = None, []
        for r0 in range(b0, b0 + HG_BLOCK, SUBLANES):
            x = g[r0:r0 + SUBLANES, :]
            for sh in (1, 2, 4):
                x = x + jnp.where(sub >= sh, pltpu.roll(x, sh, 0), 0.0)
            if carry is not None:
                x = x + carry
            carry = x[SUBLANES - 1:SUBLANES, :]
            pre.append(x)
        pre = jnp.concatenate(pre, axis=0)
        out.append(jnp.where(reverse, carry - pre + g[b0:b0 + HG_BLOCK, :], pre))
    return jnp.concatenate(out, axis=0)


def _ep_forget(acc, rs, cs, n, extras, outs, *, n_fwd):
    lg = extras[0][:, cs]
    mx = jnp.max(lg, axis=0, keepdims=True)
    e = jnp.exp(lg - mx)
    lb = e[0:1, :] / jnp.sum(e, axis=0, keepdims=True)
    f = lb + (1.0 - lb) * _sigmoid(acc)
    outs[0][rs, cs] = _block_cumsum(jnp.log(f), n >= n_fwd)
    outs[1][rs, cs] = (1.0 - f).astype(BF16)


def _inproj_kernel(h_ref, w_hbm, *rest, epilogue, n_extra, n_out, col0, tn, chunk, per_step, n_tiles, m_tiles,
                   piece_rows):
    extras = rest[:n_extra]
    outs = rest[n_extra:n_extra + n_out]
    wbf, stage, acc_scr, sem = rest[n_extra + n_out:]
    t = pl.program_id(0)
    steps = n_tiles * m_tiles
    n, m = t // m_tiles, t % m_tiles
    n_prev = jnp.maximum(t - 1, 0) // m_tiles
    k = wbf.shape[1]
    n_chunks = k // chunk

    def chunk_copy(tile, c, s):
        rows = pl.ds(pl.multiple_of(c * chunk, chunk), chunk)
        cols = pl.ds(pl.multiple_of(col0 + tile * tn, LANES), tn)
        return pltpu.make_async_copy(w_hbm.at[rows, cols], stage.at[s], sem.at[s])

    def land(tile, c, s):
        chunk_copy(tile, c, s).wait()
        rows = pl.ds(pl.multiple_of(c * chunk, chunk), chunk)
        wbf[tile % 2, rows, :] = stage[s].astype(BF16)

    @pl.when(t == 0)
    def _():
        acc_scr[...] = jnp.zeros_like(acc_scr)
        chunk_copy(0, 0, 0).start()
        for c in range(n_chunks):
            if c + 1 < n_chunks:
                chunk_copy(0, c + 1, (c + 1) % 2).start()
            land(0, c, c % 2)

    pm = jnp.where(m == 0, m_tiles - 1, m - 1)
    ptile = jnp.where(m == 0, n, n + 1)
    for i in range(per_step):
        c = pm * per_step + i

        @pl.when((t > 0) & (c < n_chunks) & (ptile < n_tiles))
        def _(c=c, i=i):
            land(ptile, c, i)

    for i in range(per_step):
        c = m * per_step + i

        @pl.when((c < n_chunks) & (n + 1 < n_tiles))
        def _(c=c, i=i):
            chunk_copy(n + 1, c, i).start()

    sw = min(MXU_N, tn)
    slices = [slice(j * sw, (j + 1) * sw) for j in range(tn // sw)]
    tm = h_ref.shape[0]
    rp = min(piece_rows, tm)
    rows = [slice(r0, r0 + rp) for r0 in range(0, tm, rp)]

    @pl.when(t < steps)
    def _():
        slot = n % 2
        for rs in rows:
            for cs in slices:
                epilogue(acc_scr[rs, cs], rs, cs, n_prev, extras, outs)
            acc_scr[rs, :] = _dot(h_ref[rs, :], wbf[slot])

    @pl.when(t == steps)
    def _():
        for cs in slices:
            for rs in rows:
                epilogue(acc_scr[rs, cs], rs, cs, n_prev, extras, outs)


def _inproj(h, w, col0, width, rows, epilogue, extras, extra_specs, out_dtypes, tm, tn, name):
    k = h.shape[1]
    n_tiles, m_tiles = width // tn, rows // tm
    steps = n_tiles * m_tiles
    chunk = min(256, k // 2)
    n_chunks = k // chunk
    per_step = -(-n_chunks // m_tiles)
    n_stage = max(2, per_step)

    def prev_tile(t):
        tp = jnp.maximum(t - 1, 0)
        return tp // m_tiles, tp % m_tiles

    def deferred(fn):
        return lambda t: fn(*prev_tile(t))

    in_specs = [pl.BlockSpec((tm, k), lambda t: (jnp.minimum(t, steps - 1) % m_tiles, 0)),
                pl.BlockSpec(memory_space=pl.ANY)]
    in_specs += [pl.BlockSpec(shape, deferred(fn)) for shape, fn in extra_specs]
    out_specs = [pl.BlockSpec((tm, tn), deferred(lambda n, m: (m, n))) for _ in out_dtypes]
    out_shape = [jax.ShapeDtypeStruct((rows, width), dt) for dt in out_dtypes]
    return pl.pallas_call(
        functools.partial(_inproj_kernel, epilogue=epilogue, n_extra=len(extras), n_out=len(out_dtypes),
                          col0=col0, tn=tn, chunk=chunk, per_step=per_step, n_tiles=n_tiles, m_tiles=m_tiles,
                          piece_rows=PIECE_ROWS),
        grid=(steps + 1,),
        in_specs=in_specs,
        out_specs=out_specs,
        out_shape=out_shape,
        scratch_shapes=[pltpu.VMEM((2, k, tn), BF16),
                        pltpu.VMEM((n_stage, chunk, tn), F32),
                        pltpu.VMEM((tm, tn), F32),
                        pltpu.SemaphoreType.DMA((n_stage,))],
        compiler_params=_params(("arbitrary",)),
        name=name,
    )(h, w, *extras)


def _col_reduce(x, op):
    r = x.shape[0]
    while r % (2 * SUBLANES) == 0 and r > SUBLANES:
        r //= 2
        x = op(x[:r], x[r:])
    return x


def _attn_kernel(q_ref, kl_ref, vl_ref, kc_ref, vc_ref, sg_ref, *rest, key_chunk, n_side):
    side_in = rest[:n_side]
    o_ref = rest[n_side]
    side_out = rest[n_side + 1:2 * n_side + 1]
    k_scr, vt_scr, st_in, st_out, sem_in, sem_out = rest[2 * n_side + 1:]
    t_len = kl_ref.shape[0]
    n_keys = k_scr.shape[0]
    slab = st_in.shape[1]
    step = (pl.program_id(0) * pl.num_programs(1) + pl.program_id(1)) * pl.num_programs(2) + pl.program_id(2)
    last_step = pl.num_programs(0) * pl.num_programs(1) * pl.num_programs(2) - 1

    def slab_in(w, at_step):
        rows = pl.ds(pl.multiple_of(at_step * slab, slab), slab)
        return pltpu.make_async_copy(side_in[w].at[rows, :], st_in.at[w], sem_in.at[w])

    def slab_out(w, at_step):
        rows = pl.ds(pl.multiple_of(at_step * slab, slab), slab)
        return pltpu.make_async_copy(st_out.at[w], side_out[w].at[rows, :], sem_out.at[w])

    for w in range(n_side):
        slab_in(w, step).start()

    @pl.when(pl.program_id(2) == 0)
    def _():
        k_scr[0:t_len, :] = kl_ref[...]
        k_scr[t_len:, :] = kc_ref[...]
        vt_scr[0:HEAD_DIM, 0:t_len] = vl_ref[...].astype(F32).T.astype(BF16)
        vt_scr[0:HEAD_DIM, t_len:] = vc_ref[...].astype(F32).T.astype(BF16)
        vt_scr[HEAD_DIM:, :] = jnp.ones((vt_scr.shape[0] - HEAD_DIM, vt_scr.shape[1]), BF16)

    heads = [slice(g * HEAD_DIM, (g + 1) * HEAD_DIM) for g in range(GROUP)]
    chunks = [slice(c * key_chunk, (c + 1) * key_chunk) for c in range(n_keys // key_chunk)]

    def scores(g, ks):
        return _dot_nt(k_scr[ks, :], q_ref[:, heads[g]])

    def weigh(s_chunk, mx, ks):
        p = jnp.exp2(s_chunk - mx).astype(BF16)
        return _dot(vt_scr[:, ks], p)

    def col_max(s_chunks):
        part = [_col_reduce(sc, jnp.maximum) for sc in s_chunks]
        return jnp.max(functools.reduce(jnp.maximum, part), axis=0, keepdims=True)

    s_cur = [scores(0, ks) for ks in chunks]
    for g in range(GROUP):
        mx = col_max(s_cur)
        s_nxt, acc = [], None
        for c, ks in enumerate(chunks):
            if g + 1 < GROUP:
                s_nxt.append(scores(g + 1, ks))
            pv = weigh(s_cur[c], mx, ks)
            acc = pv if acc is None else acc + pv
        o = acc[0:HEAD_DIM] / acc[HEAD_DIM:HEAD_DIM + 1]
        o_ref[:, heads[g]] = (o.T * sg_ref[:, heads[g]].astype(F32)).astype(BF16)
        s_cur = s_nxt

    for w in range(n_side):
        slab_in(w, step).wait()

        @pl.when(step > 0)
        def _(w=w):
            slab_out(w, step - 1).wait()

        st_out[w] = st_in[w].astype(BF16)
        slab_out(w, step).start()

        @pl.when(step == last_step)
        def _(w=w):
            slab_out(w, step).wait()


def _attention(q, kk, vv, sg, batch, t_len, c_len, tq, side):
    m_lat = batch * t_len
    n_kv = kk.shape[1] // HEAD_DIM
    gw = GROUP * HEAD_DIM
    nq = t_len // tq
    ctx0 = m_lat // c_len
    n_keys = t_len + c_len
    half = n_keys // 2
    key_chunk = half if half % LANES == 0 else n_keys
    steps = batch * n_kv * nq
    rows, cols = side[0].shape
    assert all(w.shape == (rows, cols) for w in side) and rows % (steps * 2 * SUBLANES) == 0
    slab = rows // steps
    any_spec = pl.BlockSpec(memory_space=pl.ANY)
    return pl.pallas_call(
        functools.partial(_attn_kernel, key_chunk=key_chunk, n_side=len(side)),
        grid=(batch, n_kv, nq),
        in_specs=[pl.BlockSpec((tq, gw), lambda b, h, i: (b * nq + i, h)),
                  pl.BlockSpec((t_len, HEAD_DIM), lambda b, h, i: (b, h)),
                  pl.BlockSpec((t_len, HEAD_DIM), lambda b, h, i: (b, h)),
                  pl.BlockSpec((c_len, HEAD_DIM), lambda b, h, i: (ctx0 + b, h)),
                  pl.BlockSpec((c_len, HEAD_DIM), lambda b, h, i: (ctx0 + b, h)),
                  pl.BlockSpec((tq, gw), lambda b, h, i: (b * nq + i, h))] + [any_spec] * len(side),
        out_specs=[pl.BlockSpec((tq, gw), lambda b, h, i: (b * nq + i, h))] + [any_spec] * len(side),
        out_shape=[jax.ShapeDtypeStruct((m_lat, n_kv * gw), BF16)]
                  + [jax.ShapeDtypeStruct((rows, cols), BF16) for _ in side],
        scratch_shapes=[pltpu.VMEM((n_keys, HEAD_DIM), BF16),
                        pltpu.VMEM((HEAD_DIM + 2 * SUBLANES, n_keys), BF16),
                        pltpu.VMEM((len(side), slab, cols), F32),
                        pltpu.VMEM((len(side), slab, cols), BF16),
                        pltpu.SemaphoreType.DMA((len(side),)),
                        pltpu.SemaphoreType.DMA((len(side),))],
        compiler_params=_params(("arbitrary", "arbitrary", "arbitrary")),
        name="attn",
    )(q, kk, vv, kk, vv, sg, *side)


def _hgrn_masks(c, reverse):
    ns = c // SUB
    row = lax.broadcasted_iota(jnp.int32, (c, c), 0)
    col = lax.broadcasted_iota(jnp.int32, (c, c), 1)
    pr, pc = row // SUB, col // SUB
    if reverse:
        pr, pc = ns - 1 - pr, ns - 1 - pc
        causal = col >= row
    else:
        causal = col <= row
    m_diag = (pr == pc) & causal
    m_adj = ((pr == 1) & (pc == 0)) | ((pr == 3) & (pc == 2))
    m_far = (pr >= 2) & (pc <= 1)
    return m_diag, m_adj, m_far


def _hgrn_prepare(q, k, bfull, reverse, want_out):
    c = k.shape[0]
    ns = c // SUB
    assert ns == 4
    order = list(range(ns))[::-1] if reverse else list(range(ns))
    zero = jnp.zeros((1, LANES), F32)
    start, end = {}, {}
    for i in range(ns):
        lo, hi = i * SUB, i * SUB + SUB - 1
        if reverse:
            start[i] = bfull[hi + 1:hi + 2, :] if i < ns - 1 else zero
            end[i] = bfull[lo:lo + 1, :]
        else:
            start[i] = bfull[lo - 1:lo, :] if i > 0 else zero
            end[i] = bfull[hi:hi + 1, :]
    btot = end[order[-1]]
    qt, khat, kd, q_far, k_far, qs, ke = {}, {}, {}, {}, {}, {}, {}
    for p, i in enumerate(order):
        rs = slice(i * SUB, (i + 1) * SUB)
        b_i = bfull[rs]
        kd[i] = k[rs] * jnp.exp(end[i] - b_i)
        ke[i] = kd[i] if p == ns - 1 else kd[i] * jnp.exp(btot - end[i])
        if want_out:
            bl = b_i - start[i]
            qt[i] = q[rs] * jnp.exp(bl)
            khat[i] = k[rs] * jnp.exp(-bl)
            qs[i] = qt[i] if p == 0 else qt[i] * jnp.exp(start[i])
            q_far[i] = qt[i] * jnp.exp(start[i] - start[order[2]]) if p == 3 else qt[i]
            k_far[i] = kd[i] * jnp.exp(start[order[2]] - end[i]) if p == 0 else kd[i]

    def cat(d):
        return jnp.concatenate([d[i].astype(BF16) for i in range(ns)], axis=0)

    out = {"ke": cat(ke), "dec": jnp.exp(btot)}
    if want_out:
        out.update(qt=cat(qt), kk=jnp.concatenate([cat(kd), cat(khat)], axis=0),
                   q_far=cat(q_far), k_far=cat(k_far), qs=cat(qs))
    return out


def _hgrn_kernel(*refs, reverse, n_ctx, heads, final):
    if final:
        q_ref, k_ref, b_ref, v_ref, of_ref, sg_ref, hw_ref, o_ref, st_ref = refs
    else:
        q_ref, k_ref, b_ref, v_ref, o_ref, st_ref = refs
    s = pl.program_id(2)

    @pl.when(s == 0)
    def _():
        st_ref[...] = jnp.zeros_like(st_ref)

    c = HG_BLOCK
    n_sub = k_ref.shape[0] // c
    order = list(range(n_sub))[::-1] if reverse else list(range(n_sub))
    cols = [slice(j * LANES, (j + 1) * LANES) for j in range(heads)]

    def run(want_out):
        if want_out:
            m_diag, m_adj, m_far = _hgrn_masks(c, reverse)
        for bi in order:
            rows = slice(bi * c, (bi + 1) * c)
            prep = [_hgrn_prepare(q_ref[rows, cs].astype(F32) if want_out else None,
                                  k_ref[rows, cs].astype(F32), b_ref[rows, cs], reverse, want_out)
                    for cs in cols]
            if want_out:
                near = [_dot_nt(p["qt"], p["kk"]) for p in prep]
                far = [_dot_nt(p["q_far"], p["k_far"]) for p in prep]
                inter = [_dot_nt(p["qs"], st_ref[j].astype(BF16)) for j, p in enumerate(prep)]
            for j, (cs, p) in enumerate(zip(cols, prep)):
                st_ref[j] = p["dec"] * st_ref[j] + _dot_tn(v_ref[rows, cs], p["ke"])
            if want_out:
                amat = [jnp.where(m_diag, n[:, c:], jnp.where(m_adj, n[:, :c], jnp.where(m_far, f, 0.0))).astype(BF16)
                        for n, f in zip(near, far)]
                for j, cs in enumerate(cols):
                    o = _dot(amat[j], v_ref[rows, cs]) + inter[j]
                    if final:
                        tot = o + of_ref[rows, cs]
                        y = tot * lax.rsqrt(jnp.mean(tot * tot, axis=-1, keepdims=True) + EPS)
                        y = y * hw_ref[:, cs]
                        o_ref[rows, cs] = (y * sg_ref[rows, cs].astype(F32)).astype(o_ref.dtype)
                    else:
                        o_ref[rows, cs] = o.astype(o_ref.dtype)

    pl.when(s < n_ctx)(lambda: run(False))
    pl.when(s >= n_ctx)(lambda: run(True))


def _hgrn(q, kk, bb, vv, w, batch, t_len, c_len, reverse, heads, cols, blk, final_args=None):
    m_lat = batch * t_len
    hb = heads * LANES
    n_ctx, n_lat = c_len // blk, t_len // blk
    ctx0 = m_lat // blk
    q0, k0, v0, sg0 = (cc // hb for cc in cols)

    def row_block(b, s):
        if reverse:
            cblk = ctx0 + b * n_ctx + (n_ctx - 1 - s)
            lblk = b * n_lat + (n_lat - 1 - (s - n_ctx))
        else:
            cblk = ctx0 + b * n_ctx + s
            lblk = b * n_lat + (s - n_ctx)
        return jnp.where(s < n_ctx, cblk, lblk)

    def lat_block(b, s):
        sl = jnp.maximum(s - n_ctx, 0)
        return b * n_lat + ((n_lat - 1 - sl) if reverse else sl)

    def all_spec(c0):
        return pl.BlockSpec((blk, hb), lambda b, h, s: (row_block(b, s), c0 + h))

    def lat_spec(c0):
        return pl.BlockSpec((blk, hb), lambda b, h, s: (lat_block(b, s), c0 + h))

    in_specs = [lat_spec(q0), all_spec(k0), all_spec(k0), all_spec(v0)]
    args = [q, kk, bb, vv]
    final = final_args is not None
    if final:
        o_f, sg, hw = final_args
        in_specs += [lat_spec(0), lat_spec(sg0), pl.BlockSpec((1, hb), lambda b, h, s: (0, h))]
        args += [o_f, sg, hw]
    return pl.pallas_call(
        functools.partial(_hgrn_kernel, reverse=reverse, n_ctx=n_ctx, heads=heads, final=final),
        grid=(batch, w // hb, n_ctx + n_lat),
        in_specs=in_specs,
        out_specs=lat_spec(0),
        out_shape=jax.ShapeDtypeStruct((m_lat, w), BF16 if final else F32),
        scratch_shapes=[pltpu.VMEM((heads, LANES, LANES), F32)],
        compiler_params=_params(("arbitrary", "arbitrary", "arbitrary")),
        name="hgrn_bwd" if reverse else "hgrn_fwd",
    )(*args)


def _proj_kernel(yh_ref, ya_ref, wh_ref, wa_ref, rh_ref, ra_ref, o_ref):
    a = _dot(yh_ref[...], wh_ref[...])
    b = _dot(ya_ref[...], wa_ref[...])
    o_ref[...] = (rh_ref[...].astype(F32) * a + ra_ref[...].astype(F32) * b).astype(BF16)


def _proj(yh, ya, wh, wa, gates, rh_col0, ra_col0, tm, tn):
    m, k = yh.shape
    n = wh.shape[1]
    rh0, ra0 = rh_col0 // tn, ra_col0 // tn
    return pl.pallas_call(
        _proj_kernel,
        grid=(m // tm, n // tn),
        in_specs=[pl.BlockSpec((tm, k), lambda i, j: (i, 0)),
                  pl.BlockSpec((tm, k), lambda i, j: (i, 0)),
                  pl.BlockSpec((k, tn), lambda i, j: (0, j)),
                  pl.BlockSpec((k, tn), lambda i, j: (0, j)),
                  pl.BlockSpec((tm, tn), lambda i, j: (i, rh0 + j)),
                  pl.BlockSpec((tm, tn), lambda i, j: (i, ra0 + j))],
        out_specs=pl.BlockSpec((tm, tn), lambda i, j: (i, j)),
        out_shape=jax.ShapeDtypeStruct((m, n), BF16),
        compiler_params=_params(("arbitrary", "arbitrary")),
        name="proj",
    )(yh, ya, wh, wa, gates, gates)


def _out_kernel(mg_ref, w_ref, x_ref, gate_ref, fw_ref, o_ref, xn_scr):
    n = pl.program_id(1)
    n_tiles, _, tn = xn_scr.shape
    xn_scr[n] = x_ref[...] + gate_ref[0] * _dot(mg_ref[...], w_ref[...])

    @pl.when(n == n_tiles - 1)
    def _():
        ssq = None
        for j in range(n_tiles):
            v = xn_scr[j]
            part = jnp.sum(v * v, axis=-1, keepdims=True)
            ssq = part if ssq is None else ssq + part
        inv = lax.rsqrt(ssq / (n_tiles * tn) + EPS)
        for j in range(n_tiles):
            cs = slice(j * tn, (j + 1) * tn)
            o_ref[:, cs] = xn_scr[j] * inv * fw_ref[:, cs]


def _outproj(mg, w, x2, mod3, fw, t_len, tm, tn):
    m, k = mg.shape
    n = w.shape[1]
    gate0 = 2 * n // tn
    per_b = t_len // tm
    return pl.pallas_call(
        _out_kernel,
        grid=(m // tm, n // tn),
        in_specs=[pl.BlockSpec((tm, k), lambda i, j: (i, 0)),
                  pl.BlockSpec((k, tn), lambda i, j: (0, j)),
                  pl.BlockSpec((tm, tn), lambda i, j: (i, j)),
                  pl.BlockSpec((1, 1, tn), lambda i, j: (i // per_b, 0, gate0 + j)),
                  pl.BlockSpec((1, n), lambda i, j: (0, 0))],
        out_specs=pl.BlockSpec((tm, n), lambda i, j: (i, 0)),
        out_shape=jax.ShapeDtypeStruct((m, n), F32),
        scratch_shapes=[pltpu.VMEM((n // tn, tm, tn), F32)],
        compiler_params=_params(("arbitrary", "arbitrary")),
        name="outproj",
    )(mg, w, x2, mod3, fw)


def _rope_tables(t_len, pad_rows):
    axis_dim = HEAD_DIM // 2
    rows = t_len // GRID_W
    row = jnp.repeat(jnp.arange(rows, dtype=F32), GRID_W)
    col = jnp.tile(jnp.arange(GRID_W, dtype=F32), rows)
    inv = ROPE_THETA ** (-jnp.arange(0, axis_dim, 2, dtype=F32) / axis_dim)
    fr = row[:, None] * inv[None]
    fc = col[:, None] * inv[None]
    ang = jnp.concatenate([fr, fr, fc, fc], axis=-1)
    cos, sin = jnp.cos(ang), jnp.sin(ang)
    lane = jnp.arange(HEAD_DIM)
    sign = jnp.where((lane // (HEAD_DIM // 4)) % 2 == 0, -1.0, 1.0).astype(F32)
    cos = jnp.concatenate([cos, jnp.ones((pad_rows, HEAD_DIM), F32)], axis=0)
    sin = jnp.concatenate([sin * sign, jnp.zeros((pad_rows, HEAD_DIM), F32)], axis=0)
    return cos, sin


def _tile(n, pref):
    t = min(n, pref)
    while n % t:
        t //= 2
    return t


def kernel(x, c, ctx, c_ctx, norm_w, w_ada, b_ada, w_in, q_norm_w, k_norm_w, lb_logits,
           hg_norm_w, w_proj_hg, w_proj_att, w_out, final_norm_w):
    batch, t_len, d = x.shape
    c_len = ctx.shape[1]
    depth = norm_w.shape[0]
    assert depth == 1, "single-layer block"
    assert t_len % HG_BLOCK == 0 and c_len % HG_BLOCK == 0
    att_w = d
    kv_w = d // GROUP
    m_lat, m_ctx = batch * t_len, batch * c_len
    m_all = m_lat + m_ctx

    x2 = x.reshape(m_lat, d)
    c2 = ctx.reshape(m_ctx, d)

    pad = (-(batch + 1)) % SUBLANES
    s = jnp.concatenate([c, c_ctx[None, :], jnp.zeros((pad, d), F32)], axis=0)
    mod = _ada(s, w_ada[0], b_ada[0][None, :], _tile(3 * d, 512))
    mod3 = mod.reshape(mod.shape[0], 1, 3 * d)

    h = _prenorm(x2, c2, norm_w[0][None, :], mod3, batch, _tile(c_len, 256))

    tm = _tile(m_ctx, 1024)
    tn = _tile(kv_w, 1024)
    n_lat_tiles = m_lat // tm
    per_seq = t_len // tm
    cos, sin = _rope_tables(t_len, tm)
    sw = min(MXU_N, tn)
    lane = jnp.arange(sw)
    perm = (lane[:, None] == (lane[None, :] ^ (HEAD_DIM // 4))).astype(BF16)
    perm2 = jnp.concatenate([perm, perm], axis=0)
    partner = jnp.arange(HEAD_DIM) ^ (HEAD_DIM // 4)

    def rope_inputs(nw):
        return [cos * nw[None, :], sin * nw[partner][None, :], perm2]

    def rope_row(n, m):
        return (jnp.where(m < n_lat_tiles, m % per_seq, per_seq), 0)

    rope_specs = [((tm, HEAD_DIM), rope_row), ((tm, HEAD_DIM), rope_row), ((2 * sw, sw), lambda n, m: (0, 0))]
    w = w_in[0]
    c_k, c_v, c_f, c_q, c_g = 0, kv_w, 2 * kv_w + d, 2 * kv_w + 3 * d, 2 * kv_w + 3 * d + att_w

    (k_att,) = _inproj(h, w, c_k, kv_w, m_all, functools.partial(_ep_normrope, post_scale=1.0),
                       rope_inputs(k_norm_w[0]), rope_specs, [BF16], tm, tn, "inproj_k")
    (vi,) = _inproj(h, w, c_v, kv_w + d, m_all, _ep_plain, [], [], [BF16], tm, tn, "inproj_vi")
    slots = lb_logits.shape[1]
    lb2 = jnp.transpose(lb_logits, (1, 0, 2)).reshape(slots, 2 * d)
    b_all, k_all = _inproj(h, w, c_f, 2 * d, m_all, functools.partial(_ep_forget, n_fwd=d // tn),
                           [lb2], [((slots, tn), lambda n, m: (0, n))], [F32, BF16], tm, tn, "inproj_f")
    (q_att,) = _inproj(h, w, c_q, att_w, m_lat,
                       functools.partial(_ep_normrope, post_scale=HEAD_DIM ** -0.5 * LOG2E),
                       rope_inputs(q_norm_w[0]), rope_specs, [BF16], tm, tn, "inproj_q")
    (gates,) = _inproj(h, w, c_g, att_w + 4 * d, m_lat,
                       functools.partial(_ep_gate, n_silu=(att_w + 2 * d) // tn),
                       [], [], [BF16], tm, tn, "inproj_g")
    g_qh, g_gh, g_rh, g_ra = att_w, att_w + d, att_w + 2 * d, att_w + 3 * d

    y_att, wh, wa, wo = _attention(q_att, k_att, vi, gates, batch, t_len, c_len, _tile(t_len, 1024),
                                   [w_proj_hg[0], w_proj_att[0], w_out[0]])

    heads = _tile(kv_w // LANES, 8)
    blk = _tile(c_len, 2 * HG_BLOCK)
    o_f = _hgrn(gates, k_all, b_all, vi, d, batch, t_len, c_len, False, heads, (g_qh, 0, kv_w, 0), blk)
    y_hg = _hgrn(gates, k_all, b_all, vi, d, batch, t_len, c_len, True, heads, (g_qh, d, kv_w, g_gh), blk,
                 final_args=(o_f, gates, hg_norm_w[0][None, :]))

    tm2 = _tile(t_len, 512)
    tn2 = _tile(d, 512)
    merged = _proj(y_hg, y_att, wh, wa, gates, g_rh, g_ra, tm2, tn2)
    out = _outproj(merged, wo, x2, mod3, final_norm_w[None, :], t_len, tm2, tn2)
    return out.reshape(batch, t_len, d)
```

```python
import functools

import jax
import jax.numpy as jnp
from jax import lax
from jax.experimental import pallas as pl
from jax.experimental.pallas import tpu as pltpu

F32 = jnp.float32
BF16 = jnp.bfloat16

HEAD_DIM = 128
GROUP = 4
GRID_W = 64
ROPE_THETA = 10000.0
EPS = 1e-6
SUB = 32
HG_BLOCK = 4 * SUB
LANES = 128
SUBLANES = 8
MXU_N = 256
LOG2E = 1.4426950408889634
PIECE_ROWS = 256
VMEM_LIMIT = 56 * 1024 * 1024


def _params(sem):
    return pltpu.CompilerParams(dimension_semantics=sem, vmem_limit_bytes=VMEM_LIMIT)


def _sigmoid(x):
    return 1.0 / (1.0 + jnp.exp(-x))


def _dot(a, b):
    return jnp.dot(a, b, preferred_element_type=F32)


def _dot_nt(a, b):
    return lax.dot_general(a, b, (((1,), (1,)), ((), ())), preferred_element_type=F32)


def _dot_tn(a, b):
    return lax.dot_general(a, b, (((0,), (0,)), ((), ())), preferred_element_type=F32)


def _split_bf16(x):
    hi = x.astype(BF16)
    lo = (x - hi.astype(F32)).astype(BF16)
    return hi, lo


def _ada_kernel(s_ref, w_ref, b_ref, o_ref):
    s = s_ref[...]
    s = s * _sigmoid(s)
    s_hi, s_lo = _split_bf16(s)
    w_hi, w_lo = _split_bf16(w_ref[...])
    rows = s.shape[0]
    r = _dot(jnp.concatenate([s_hi, s_lo], axis=0), w_hi)
    o_ref[...] = r[:rows] + r[rows:] + _dot(s_hi, w_lo) + b_ref[...]


def _ada(s, w, b, tn):
    rows, d = s.shape
    n = w.shape[1]
    return pl.pallas_call(
        _ada_kernel,
        grid=(n // tn,),
        in_specs=[pl.BlockSpec((rows, d), lambda j: (0, 0)),
                  pl.BlockSpec((d, tn), lambda j: (0, j)),
                  pl.BlockSpec((1, tn), lambda j: (0, j))],
        out_specs=pl.BlockSpec((rows, tn), lambda j: (0, j)),
        out_shape=jax.ShapeDtypeStruct((rows, n), F32),
        compiler_params=_params(("arbitrary",)),
        name="ada",
    )(s, w, b)


def _prenorm_kernel(x_ref, c_ref, nw_ref, mod_ref, h_ref, *, n_lat, d):
    def emit(src):
        x = src[...]
        y = x * lax.rsqrt(jnp.mean(x * x, axis=-1, keepdims=True) + EPS) * nw_ref[...]
        shift = mod_ref[0, :, 0:d]
        scale = mod_ref[0, :, d:2 * d]
        h_ref[...] = (y * (1.0 + scale) + shift).astype(BF16)

    i = pl.program_id(0)
    pl.when(i < n_lat)(lambda: emit(x_ref))
    pl.when(i >= n_lat)(lambda: emit(c_ref))


def _prenorm(x2, c2, nw, mod3, batch, tm):
    m_lat, d = x2.shape
    m_ctx = c2.shape[0]
    n_lat, n_ctx = m_lat // tm, m_ctx // tm
    per_b = n_lat // batch
    return pl.pallas_call(
        functools.partial(_prenorm_kernel, n_lat=n_lat, d=d),
        grid=(n_lat + n_ctx,),
        in_specs=[pl.BlockSpec((tm, d), lambda i: (jnp.minimum(i, n_lat - 1), 0)),
                  pl.BlockSpec((tm, d), lambda i: (jnp.maximum(i - n_lat, 0), 0)),
                  pl.BlockSpec((1, d), lambda i: (0, 0)),
                  pl.BlockSpec((1, 1, 3 * d), lambda i: (jnp.minimum(i // per_b, batch), 0, 0))],
        out_specs=pl.BlockSpec((tm, d), lambda i: (i, 0)),
        out_shape=jax.ShapeDtypeStruct((m_lat + m_ctx, d), BF16),
        compiler_params=_params(("arbitrary",)),
        name="prenorm",
    )(x2, c2, nw, mod3)


def _ep_plain(acc, rs, cs, n, extras, outs):
    outs[0][rs, cs] = acc.astype(BF16)


def _ep_gate(acc, rs, cs, n, extras, outs, *, n_silu):
    sg = _sigmoid(acc)
    outs[0][rs, cs] = (sg * jnp.where(n < n_silu, acc, 1.0)).astype(BF16)


def _ep_normrope(acc, rs, cs, n, extras, outs, *, post_scale):
    cos_ref, sin_ref, perm_ref = extras
    cos = cos_ref[rs, :]
    sin = sin_ref[rs, :]
    a_hi, a_lo = _split_bf16(acc)
    partner = _dot(jnp.concatenate([a_hi, a_lo], axis=1), perm_ref[...])
    for hh in range(acc.shape[1] // HEAD_DIM):
        hs = slice(hh * HEAD_DIM, (hh + 1) * HEAD_DIM)
        a = acc[:, hs]
        inv = lax.rsqrt(jnp.mean(a * a, axis=-1, keepdims=True) + EPS) * post_scale
        o = (a * cos + partner[:, hs] * sin) * inv
        outs[0][rs, cs.start + hh * HEAD_DIM:cs.start + (hh + 1) * HEAD_DIM] = o.astype(BF16)


def _block_cumsum(g, reverse):
    r, w = g.shape
    sub = lax.broadcasted_iota(jnp.int32, (SUBLANES, w), 0)
    out = []
    for b0 in range(0, r, HG_BLOCK):
        carry, pre = None, []
        for r0 in range(b0, b0 + HG_BLOCK, SUBLANES):
            x = g[r0:r0 + SUBLANES, :]
            for sh in (1, 2, 4):
                x = x + jnp.where(sub >= sh, pltpu.roll(x, sh, 0), 0.0)
            if carry is not None:
                x = x + carry
            carry = x[SUBLANES - 1:SUBLANES, :]
            pre.append(x)
        pre = jnp.concatenate(pre, axis=0)
        out.append(jnp.where(reverse, carry - pre + g[b0:b0 + HG_BLOCK, :], pre))
    return jnp.concatenate(out, axis=0)


def _ep_forget(acc, rs, cs, n, extras, outs, *, n_fwd):
    lg = extras[0][:, cs]
    mx = jnp.max(lg, axis=0, keepdims=True)
    e = jnp.exp(lg - mx)
    lb = e[0:1, :] / jnp.sum(e, axis=0, keepdims=True)
    f = lb + (1.0 - lb) * _sigmoid(acc)
    outs[0][rs, cs] = _block_cumsum(jnp.log(f), n >= n_fwd)
    outs[1][rs, cs] = (1.0 - f).astype(BF16)


def _inproj_kernel(h_ref, w_hbm, *rest, epilogue, n_extra, n_out, col0, tn, chunk, per_step, n_tiles, m_tiles,
                   piece_rows):
    extras = rest[:n_extra]
    outs = rest[n_extra:n_extra + n_out]
    wbf, stage, acc_scr, sem = rest[n_extra + n_out:]
    t = pl.program_id(0)
    steps = n_tiles * m_tiles
    n, m = t // m_tiles, t % m_tiles
    n_prev = jnp.maximum(t - 1, 0) // m_tiles
    k = wbf.shape[1]
    n_chunks = k // chunk

    def chunk_copy(tile, c, s):
        rows = pl.ds(pl.multiple_of(c * chunk, chunk), chunk)
        cols = pl.ds(pl.multiple_of(col0 + tile * tn, LANES), tn)
        return pltpu.make_async_copy(w_hbm.at[rows, cols], stage.at[s], sem.at[s])

    def land(tile, c, s):
        chunk_copy(tile, c, s).wait()
        rows = pl.ds(pl.multiple_of(c * chunk, chunk), chunk)
        wbf[tile % 2, rows, :] = stage[s].astype(BF16)

    @pl.when(t == 0)
    def _():
        acc_scr[...] = jnp.zeros_like(acc_scr)
        chunk_copy(0, 0, 0).start()
        for c in range(n_chunks):
            if c + 1 < n_chunks:
                chunk_copy(0, c + 1, (c + 1) % 2).start()
            land(0, c, c % 2)

    pm = jnp.where(m == 0, m_tiles - 1, m - 1)
    ptile = jnp.where(m == 0, n, n + 1)
    for i in range(per_step):
        c = pm * per_step + i

        @pl.when((t > 0) & (c < n_chunks) & (ptile < n_tiles))
        def _(c=c, i=i):
            land(ptile, c, i)

    for i in range(per_step):
        c = m * per_step + i

        @pl.when((c < n_chunks) & (n + 1 < n_tiles))
        def _(c=c, i=i):
            chunk_copy(n + 1, c, i).start()

    sw = min(MXU_N, tn)
    slices = [slice(j * sw, (j + 1) * sw) for j in range(tn // sw)]
    tm = h_ref.shape[0]
    rp = min(piece_rows, tm)
    rows = [slice(r0, r0 + rp) for r0 in range(0, tm, rp)]

    @pl.when(t < steps)
    def _():
        slot = n % 2
        for rs in rows:
            for cs in slices:
                epilogue(acc_scr[rs, cs], rs, cs, n_prev, extras, outs)
            acc_scr[rs, :] = _dot(h_ref[rs, :], wbf[slot])

    @pl.when(t == steps)
    def _():
        for cs in slices:
            for rs in rows:
                epilogue(acc_scr[rs, cs], rs, cs, n_prev, extras, outs)


def _inproj(h, w, col0, width, rows, epilogue, extras, extra_specs, out_dtypes, tm, tn, name):
    k = h.shape[1]
    n_tiles, m_tiles = width // tn, rows // tm
    steps = n_tiles * m_tiles
    chunk = min(256, k // 2)
    n_chunks = k // chunk
    per_step = -(-n_chunks // m_tiles)
    n_stage = max(2, per_step)

    def prev_tile(t):
        tp = jnp.maximum(t - 1, 0)
        return tp // m_tiles, tp % m_tiles

    def deferred(fn):
        return lambda t: fn(*prev_tile(t))

    in_specs = [pl.BlockSpec((tm, k), lambda t: (jnp.minimum(t, steps - 1) % m_tiles, 0)),
                pl.BlockSpec(memory_space=pl.ANY)]
    in_specs += [pl.BlockSpec(shape, deferred(fn)) for shape, fn in extra_specs]
    out_specs = [pl.BlockSpec((tm, tn), deferred(lambda n, m: (m, n))) for _ in out_dtypes]
    out_shape = [jax.ShapeDtypeStruct((rows, width), dt) for dt in out_dtypes]
    return pl.pallas_call(
        functools.partial(_inproj_kernel, epilogue=epilogue, n_extra=len(extras), n_out=len(out_dtypes),
                          col0=col0, tn=tn, chunk=chunk, per_step=per_step, n_tiles=n_tiles, m_tiles=m_tiles,
                          piece_rows=PIECE_ROWS),
        grid=(steps + 1,),
        in_specs=in_specs,
        out_specs=out_specs,
        out_shape=out_shape,
        scratch_shapes=[pltpu.VMEM((2, k, tn), BF16),
                        pltpu.VMEM((n_stage, chunk, tn), F32),
                        pltpu.VMEM((tm, tn), F32),
                        pltpu.SemaphoreType.DMA((n_stage,))],
        compiler_params=_params(("arbitrary",)),
        name=name,
    )(h, w, *extras)


def _col_reduce(x, op):
    r = x.shape[0]
    while r % (2 * SUBLANES) == 0 and r > SUBLANES:
        r //= 2
        x = op(x[:r], x[r:])
    return x


def _attn_kernel(q_ref, kl_ref, vl_ref, kc_ref, vc_ref, sg_ref, *rest, key_chunk, n_side):
    side_in = rest[:n_side]
    o_ref = rest[n_side]
    side_out = rest[n_side + 1:2 * n_side + 1]
    k_scr, vt_scr, st_in, st_out, sem_in, sem_out = rest[2 * n_side + 1:]
    t_len = kl_ref.shape[0]
    n_keys = k_scr.shape[0]
    slab = st_in.shape[1]
    step = (pl.program_id(0) * pl.num_programs(1) + pl.program_id(1)) * pl.num_programs(2) + pl.program_id(2)
    last_step = pl.num_programs(0) * pl.num_programs(1) * pl.num_programs(2) - 1

    def slab_in(w, at_step):
        rows = pl.ds(pl.multiple_of(at_step * slab, slab), slab)
        return pltpu.make_async_copy(side_in[w].at[rows, :], st_in.at[w], sem_in.at[w])

    def slab_out(w, at_step):
        rows = pl.ds(pl.multiple_of(at_step * slab, slab), slab)
        return pltpu.make_async_copy(st_out.at[w], side_out[w].at[rows, :], sem_out.at[w])

    for w in range(n_side):
        slab_in(w, step).start()

    @pl.when(pl.program_id(2) == 0)
    def _():
        k_scr[0:t_len, :] = kl_ref[...]
        k_scr[t_len:, :] = kc_ref[...]
        vt_scr[0:HEAD_DIM, 0:t_len] = vl_ref[...].astype(F32).T.astype(BF16)
        vt_scr[0:HEAD_DIM, t_len:] = vc_ref[...].astype(F32).T.astype(BF16)
        vt_scr[HEAD_DIM:, :] = jnp.ones((vt_scr.shape[0] - HEAD_DIM, vt_scr.shape[1]), BF16)

    heads = [slice(g * HEAD_DIM, (g + 1) * HEAD_DIM) for g in range(GROUP)]
    chunks = [slice(c * key_chunk, (c + 1) * key_chunk) for c in range(n_keys // key_chunk)]

    def scores(g, ks):
        return _dot_nt(k_scr[ks, :], q_ref[:, heads[g]])

    def weigh(s_chunk, mx, ks):
        p = jnp.exp2(s_chunk - mx).astype(BF16)
        return _dot(vt_scr[:, ks], p)

    def col_max(s_chunks):
        part = [_col_reduce(sc, jnp.maximum) for sc in s_chunks]
        return jnp.max(functools.reduce(jnp.maximum, part), axis=0, keepdims=True)

    s_cur = [scores(0, ks) for ks in chunks]
    for g in range(GROUP):
        mx = col_max(s_cur)
        s_nxt, acc = [], None
        for c, ks in enumerate(chunks):
            if g + 1 < GROUP:
                s_nxt.append(scores(g + 1, ks))
            pv = weigh(s_cur[c], mx, ks)
            acc = pv if acc is None else acc + pv
        o = acc[0:HEAD_DIM] / acc[HEAD_DIM:HEAD_DIM + 1]
        o_ref[:, heads[g]] = (o.T * sg_ref[:, heads[g]].astype(F32)).astype(BF16)
        s_cur = s_nxt

    for w in range(n_side):
        slab_in(w, step).wait()

        @pl.when(step > 0)
        def _(w=w):
            slab_out(w, step - 1).wait()

        st_out[w] = st_in[w].astype(BF16)
        slab_out(w, step).start()

        @pl.when(step == last_step)
        def _(w=w):
            slab_out(w, step).wait()


def _attention(q, kk, vv, sg, batch, t_len, c_len, tq, side):
    m_lat = batch * t_len
    n_kv = kk.shape[1] // HEAD_DIM
    gw = GROUP * HEAD_DIM
    nq = t_len // tq
    ctx0 = m_lat // c_len
    n_keys = t_len + c_len
    half = n_keys // 2
    key_chunk = half if half % LANES == 0 else n_keys
    steps = batch * n_kv * nq
    rows, cols = side[0].shape
    assert all(w.shape == (rows, cols) for w in side) and rows % (steps * 2 * SUBLANES) == 0
    slab = rows // steps
    any_spec = pl.BlockSpec(memory_space=pl.ANY)
    return pl.pallas_call(
        functools.partial(_attn_kernel, key_chunk=key_chunk, n_side=len(side)),
        grid=(batch, n_kv, nq),
        in_specs=[pl.BlockSpec((tq, gw), lambda b, h, i: (b * nq + i, h)),
                  pl.BlockSpec((t_len, HEAD_DIM), lambda b, h, i: (b, h)),
                  pl.BlockSpec((t_len, HEAD_DIM), lambda b, h, i: (b, h)),
                  pl.BlockSpec((c_len, HEAD_DIM), lambda b, h, i: (ctx0 + b, h)),
                  pl.BlockSpec((c_len, HEAD_DIM), lambda b, h, i: (ctx0 + b, h)),
                  pl.BlockSpec((tq, gw), lambda b, h, i: (b * nq + i, h))] + [any_spec] * len(side),
        out_specs=[pl.BlockSpec((tq, gw), lambda b, h, i: (b * nq + i, h))] + [any_spec] * len(side),
        out_shape=[jax.ShapeDtypeStruct((m_lat, n_kv * gw), BF16)]
                  + [jax.ShapeDtypeStruct((rows, cols), BF16) for _ in side],
        scratch_shapes=[pltpu.VMEM((n_keys, HEAD_DIM), BF16),
                        pltpu.VMEM((HEAD_DIM + 2 * SUBLANES, n_keys), BF16),
                        pltpu.VMEM((len(side), slab, cols), F32),
                        pltpu.VMEM((len(side), slab, cols), BF16),
                        pltpu.SemaphoreType.DMA((len(side),)),
                        pltpu.SemaphoreType.DMA((len(side),))],
        compiler_params=_params(("arbitrary", "arbitrary", "arbitrary")),
        name="attn",
    )(q, kk, vv, kk, vv, sg, *side)


def _hgrn_masks(c, reverse):
    ns = c // SUB
    row = lax.broadcasted_iota(jnp.int32, (c, c), 0)
    col = lax.broadcasted_iota(jnp.int32, (c, c), 1)
    pr, pc = row // SUB, col // SUB
    if reverse:
        pr, pc = ns - 1 - pr, ns - 1 - pc
        causal = col >= row
    else:
        causal = col <= row
    m_diag = (pr == pc) & causal
    m_adj = ((pr == 1) & (pc == 0)) | ((pr == 3) & (pc == 2))
    m_far = (pr >= 2) & (pc <= 1)
    return m_diag, m_adj, m_far


def _hgrn_prepare(q, k, bfull, reverse, want_out):
    c = k.shape[0]
    ns = c // SUB
    assert ns == 4
    order = list(range(ns))[::-1] if reverse else list(range(ns))
    zero = jnp.zeros((1, LANES), F32)
    start, end = {}, {}
    for i in range(ns):
        lo, hi = i * SUB, i * SUB + SUB - 1
        if reverse:
            start[i] = bfull[hi + 1:hi + 2, :] if i < ns - 1 else zero
            end[i] = bfull[lo:lo + 1, :]
        else:
            start[i] = bfull[lo - 1:lo, :] if i > 0 else zero
            end[i] = bfull[hi:hi + 1, :]
    btot = end[order[-1]]
    qt, khat, kd, q_far, k_far, qs, ke = {}, {}, {}, {}, {}, {}, {}
    for p, i in enumerate(order):
        rs = slice(i * SUB, (i + 1) * SUB)
        b_i = bfull[rs]
        kd[i] = k[rs] * jnp.exp(end[i] - b_i)
        ke[i] = kd[i] if p == ns - 1 else kd[i] * jnp.exp(btot - end[i])
        if want_out:
            bl = b_i - start[i]
            qt[i] = q[rs] * jnp.exp(bl)
            khat[i] = k[rs] * jnp.exp(-bl)
            qs[i] = qt[i] if p == 0 else qt[i] * jnp.exp(start[i])
            q_far[i] = qt[i] * jnp.exp(start[i] - start[order[2]]) if p == 3 else qt[i]
            k_far[i] = kd[i] * jnp.exp(start[order[2]] - end[i]) if p == 0 else kd[i]

    def cat(d):
        return jnp.concatenate([d[i].astype(BF16) for i in range(ns)], axis=0)

    out = {"ke": cat(ke), "dec": jnp.exp(btot)}
    if want_out:
        out.update(qt=cat(qt), kk=jnp.concatenate([cat(kd), cat(khat)], axis=0),
                   q_far=cat(q_far), k_far=cat(k_far), qs=cat(qs))
    return out


def _hgrn_kernel(*refs, reverse, n_ctx, heads, final):
    if final:
        q_ref, k_ref, b_ref, v_ref, of_ref, sg_ref, hw_ref, o_ref, st_ref = refs
    else:
        q_ref, k_ref, b_ref, v_ref, o_ref, st_ref = refs
    s = pl.program_id(2)

    @pl.when(s == 0)
    def _():
        st_ref[...] = jnp.zeros_like(st_ref)

    c = HG_BLOCK
    n_sub = k_ref.shape[0] // c
    order = list(range(n_sub))[::-1] if reverse else list(range(n_sub))
    cols = [slice(j * LANES, (j + 1) * LANES) for j in range(heads)]

    def run(want_out):
        if want_out:
            m_diag, m_adj, m_far = _hgrn_masks(c, reverse)
        for bi in order:
            rows = slice(bi * c, (bi + 1) * c)
            prep = [_hgrn_prepare(q_ref[rows, cs].astype(F32) if want_out else None,
                                  k_ref[rows, cs].astype(F32), b_ref[rows, cs], reverse, want_out)
                    for cs in cols]
            if want_out:
                near = [_dot_nt(p["qt"], p["kk"]) for p in prep]
                far = [_dot_nt(p["q_far"], p["k_far"]) for p in prep]
                inter = [_dot_nt(p["qs"], st_ref[j].astype(BF16)) for j, p in enumerate(prep)]
            for j, (cs, p) in enumerate(zip(cols, prep)):
                st_ref[j] = p["dec"] * st_ref[j] + _dot_tn(v_ref[rows, cs], p["ke"])
            if want_out:
                amat = [jnp.where(m_diag, n[:, c:], jnp.where(m_adj, n[:, :c], jnp.where(m_far, f, 0.0))).astype(BF16)
                        for n, f in zip(near, far)]
                for j, cs in enumerate(cols):
                    o = _dot(amat[j], v_ref[rows, cs]) + inter[j]
                    if final:
                        tot = o + of_ref[rows, cs]
                        y = tot * lax.rsqrt(jnp.mean(tot * tot, axis=-1, keepdims=True) + EPS)
                        y = y * hw_ref[:, cs]
                        o_ref[rows, cs] = (y * sg_ref[rows, cs].astype(F32)).astype(o_ref.dtype)
                    else:
                        o_ref[rows, cs] = o.astype(o_ref.dtype)

    pl.when(s < n_ctx)(lambda: run(False))
    pl.when(s >= n_ctx)(lambda: run(True))


def _hgrn(q, kk, bb, vv, w, batch, t_len, c_len, reverse, heads, cols, blk, final_args=None):
    m_lat = batch * t_len
    hb = heads * LANES
    n_ctx, n_lat = c_len // blk, t_len // blk
    ctx0 = m_lat // blk
    q0, k0, v0, sg0 = (cc // hb for cc in cols)

    def row_block(b, s):
        if reverse:
            cblk = ctx0 + b * n_ctx + (n_ctx - 1 - s)
            lblk = b * n_lat + (n_lat - 1 - (s - n_ctx))
        else:
            cblk = ctx0 + b * n_ctx + s
            lblk = b * n_lat + (s - n_ctx)
        return jnp.where(s < n_ctx, cblk, lblk)

    def lat_block(b, s):
        sl = jnp.maximum(s - n_ctx, 0)
        return b * n_lat + ((n_lat - 1 - sl) if reverse else sl)

    def all_spec(c0):
        return pl.BlockSpec((blk, hb), lambda b, h, s: (row_block(b, s), c0 + h))

    def lat_spec(c0):
        return pl.BlockSpec((blk, hb), lambda b, h, s: (lat_block(b, s), c0 + h))

    in_specs = [lat_spec(q0), all_spec(k0), all_spec(k0), all_spec(v0)]
    args = [q, kk, bb, vv]
    final = final_args is not None
    if final:
        o_f, sg, hw = final_args
        in_specs += [lat_spec(0), lat_spec(sg0), pl.BlockSpec((1, hb), lambda b, h, s: (0, h))]
        args += [o_f, sg, hw]
    return pl.pallas_call(
        functools.partial(_hgrn_kernel, reverse=reverse, n_ctx=n_ctx, heads=heads, final=final),
        grid=(batch, w // hb, n_ctx + n_lat),
        in_specs=in_specs,
        out_specs=lat_spec(0),
        out_shape=jax.ShapeDtypeStruct((m_lat, w), BF16 if final else F32),
        scratch_shapes=[pltpu.VMEM((heads, LANES, LANES), F32)],
        compiler_params=_params(("arbitrary", "arbitrary", "arbitrary")),
        name="hgrn_bwd" if reverse else "hgrn_fwd",
    )(*args)


def _proj_kernel(yh_ref, ya_ref, wh_ref, wa_ref, rh_ref, ra_ref, o_ref):
    a = _dot(yh_ref[...], wh_ref[...])
    b = _dot(ya_ref[...], wa_ref[...])
    o_ref[...] = (rh_ref[...].astype(F32) * a + ra_ref[...].astype(F32) * b).astype(BF16)


def _proj(yh, ya, wh, wa, gates, rh_col0, ra_col0, tm, tn):
    m, k = yh.shape
    n = wh.shape[1]
    rh0, ra0 = rh_col0 // tn, ra_col0 // tn
    return pl.pallas_call(
        _proj_kernel,
        grid=(m // tm, n // tn),
        in_specs=[pl.BlockSpec((tm, k), lambda i, j: (i, 0)),
                  pl.BlockSpec((tm, k), lambda i, j: (i, 0)),
                  pl.BlockSpec((k, tn), lambda i, j: (0, j)),
                  pl.BlockSpec((k, tn), lambda i, j: (0, j)),
                  pl.BlockSpec((tm, tn), lambda i, j: (i, rh0 + j)),
                  pl.BlockSpec((tm, tn), lambda i, j: (i, ra0 + j))],
        out_specs=pl.BlockSpec((tm, tn), lambda i, j: (i, j)),
        out_shape=jax.ShapeDtypeStruct((m, n), BF16),
        compiler_params=_params(("arbitrary", "arbitrary")),
        name="proj",
    )(yh, ya, wh, wa, gates, gates)


def _out_kernel(mg_ref, w_ref, x_ref, gate_ref, fw_ref, o_ref, xn_scr):
    n = pl.program_id(1)
    n_tiles, _, tn = xn_scr.shape
    xn_scr[n] = x_ref[...] + gate_ref[0] * _dot(mg_ref[...], w_ref[...])

    @pl.when(n == n_tiles - 1)
    def _():
        ssq = None
        for j in range(n_tiles):
            v = xn_scr[j]
            part = jnp.sum(v * v, axis=-1, keepdims=True)
            ssq = part if ssq is None else ssq + part
        inv = lax.rsqrt(ssq / (n_tiles * tn) + EPS)
        for j in range(n_tiles):
            cs = slice(j * tn, (j + 1) * tn)
            o_ref[:, cs] = xn_scr[j] * inv * fw_ref[:, cs]


def _outproj(mg, w, x2, mod3, fw, t_len, tm, tn):
    m, k = mg.shape
    n = w.shape[1]
    gate0 = 2 * n // tn
    per_b = t_len // tm
    return pl.pallas_call(
        _out_kernel,
        grid=(m // tm, n // tn),
        in_specs=[pl.BlockSpec((tm, k), lambda i, j: (i, 0)),
                  pl.BlockSpec((k, tn), lambda i, j: (0, j)),
                  pl.BlockSpec((tm, tn), lambda i, j: (i, j)),
                  pl.BlockSpec((1, 1, tn), lambda i, j: (i // per_b, 0, gate0 + j)),
                  pl.BlockSpec((1, n), lambda i, j: (0, 0))],
        out_specs=pl.BlockSpec((tm, n), lambda i, j: (i, 0)),
        out_shape=jax.ShapeDtypeStruct((m, n), F32),
        scratch_shapes=[pltpu.VMEM((n // tn, tm, tn), F32)],
        compiler_params=_params(("arbitrary", "arbitrary")),
        name="outproj",
    )(mg, w, x2, mod3, fw)


def _rope_tables(t_len, pad_rows):
    axis_dim = HEAD_DIM // 2
    rows = t_len // GRID_W
    row = jnp.repeat(jnp.arange(rows, dtype=F32), GRID_W)
    col = jnp.tile(jnp.arange(GRID_W, dtype=F32), rows)
    inv = ROPE_THETA ** (-jnp.arange(0, axis_dim, 2, dtype=F32) / axis_dim)
    fr = row[:, None] * inv[None]
    fc = col[:, None] * inv[None]
    ang = jnp.concatenate([fr, fr, fc, fc], axis=-1)
    cos, sin = jnp.cos(ang), jnp.sin(ang)
    lane = jnp.arange(HEAD_DIM)
    sign = jnp.where((lane // (HEAD_DIM // 4)) % 2 == 0, -1.0, 1.0).astype(F32)
    cos = jnp.concatenate([cos, jnp.ones((pad_rows, HEAD_DIM), F32)], axis=0)
    sin = jnp.concatenate([sin * sign, jnp.zeros((pad_rows, HEAD_DIM), F32)], axis=0)
    return cos, sin


def _tile(n, pref):
    t = min(n, pref)
    while n % t:
        t //= 2
    return t


def kernel(x, c, ctx, c_ctx, norm_w, w_ada, b_ada, w_in, q_norm_w, k_norm_w, lb_logits,
           hg_norm_w, w_proj_hg, w_proj_att, w_out, final_norm_w):
    batch, t_len, d = x.shape
    c_len = ctx.shape[1]
    depth = norm_w.shape[0]
    assert depth == 1, "single-layer block"
    assert t_len % HG_BLOCK == 0 and c_len % HG_BLOCK == 0
    att_w = d
    kv_w = d // GROUP
    m_lat, m_ctx = batch * t_len, batch * c_len
    m_all = m_lat + m_ctx

    x2 = x.reshape(m_lat, d)
    c2 = ctx.reshape(m_ctx, d)

    pad = (-(batch + 1)) % SUBLANES
    s = jnp.concatenate([c, c_ctx[None, :], jnp.zeros((pad, d), F32)], axis=0)
    mod = _ada(s, w_ada[0], b_ada[0][None, :], _tile(3 * d, 512))
    mod3 = mod.reshape(mod.shape[0], 1, 3 * d)

    h = _prenorm(x2, c2, norm_w[0][None, :], mod3, batch, _tile(c_len, 256))

    tm = _tile(m_ctx, 1024)
    tn = _tile(kv_w, 1024)
    n_lat_tiles = m_lat // tm
    per_seq = t_len // tm
    cos, sin = _rope_tables(t_len, tm)
    sw = min(MXU_N, tn)
    lane = jnp.arange(sw)
    perm = (lane[:, None] == (lane[None, :] ^ (HEAD_DIM // 4))).astype(BF16)
    perm2 = jnp.concatenate([perm, perm], axis=0)
    partner = jnp.arange(HEAD_DIM) ^ (HEAD_DIM // 4)

    def rope_inputs(nw):
        return [cos * nw[None, :], sin * nw[partner][None, :], perm2]

    def rope_row(n, m):
        return (jnp.where(m < n_lat_tiles, m % per_seq, per_seq), 0)

    rope_specs = [((tm, HEAD_DIM), rope_row), ((tm, HEAD_DIM), rope_row), ((2 * sw, sw), lambda n, m: (0, 0))]
    w = w_in[0]
    c_k, c_v, c_f, c_q, c_g = 0, kv_w, 2 * kv_w + d, 2 * kv_w + 3 * d, 2 * kv_w + 3 * d + att_w

    (k_att,) = _inproj(h, w, c_k, kv_w, m_all, functools.partial(_ep_normrope, post_scale=1.0),
                       rope_inputs(k_norm_w[0]), rope_specs, [BF16], tm, tn, "inproj_k")
    (v_att,) = _inproj(h, w, c_v, kv_w, m_all, _ep_plain, [], [], [BF16], tm, tn, "inproj_v")
    (i_hg,) = _inproj(h, w, c_v + kv_w, d, m_all, _ep_plain, [], [], [BF16], tm, tn, "inproj_i")
    slots = lb_logits.shape[1]
    lb2 = jnp.transpose(lb_logits, (1, 0, 2)).reshape(slots, 2 * d)
    b_all, k_all = _inproj(h, w, c_f, 2 * d, m_all, functools.partial(_ep_forget, n_fwd=d // tn),
                           [lb2], [((slots, tn), lambda n, m: (0, n))], [F32, BF16], tm, tn, "inproj_f")
    (q_att,) = _inproj(h, w, c_q, att_w, m_lat,
                       functools.partial(_ep_normrope, post_scale=HEAD_DIM ** -0.5 * LOG2E),
                       rope_inputs(q_norm_w[0]), rope_specs, [BF16], tm, tn, "inproj_q")
    (gates,) = _inproj(h, w, c_g, att_w + 4 * d, m_lat,
                       functools.partial(_ep_gate, n_silu=(att_w + 2 * d) // tn),
                       [], [], [BF16], tm, tn, "inproj_g")
    g_qh, g_gh, g_rh, g_ra = att_w, att_w + d, att_w + 2 * d, att_w + 3 * d

    y_att, wh, wa, wo = _attention(q_att, k_att, v_att, gates, batch, t_len, c_len, _tile(t_len, 1024),
                                   [w_proj_hg[0], w_proj_att[0], w_out[0]])

    heads = _tile(d // LANES, 16)
    blk = _tile(c_len, 2 * HG_BLOCK)
    o_f = _hgrn(gates, k_all, b_all, i_hg, d, batch, t_len, c_len, False, heads, (g_qh, 0, 0, 0), blk)
    y_hg = _hgrn(gates, k_all, b_all, i_hg, d, batch, t_len, c_len, True, heads, (g_qh, d, 0, g_gh), blk,
                 final_args=(o_f, gates, hg_norm_w[0][None, :]))

    tm2 = _tile(t_len, 512)
    tn2 = _tile(d, 512)
    merged = _proj(y_hg, y_att, wh, wa, gates, g_rh, g_ra, tm2, tn2)
    out = _outproj(merged, wo, x2, mod3, final_norm_w[None, :], t_len, tm2, tn2)
    return out.reshape(batch, t_len, d)
```

```python
import functools
import math

import jax
import jax.numpy as jnp
from jax import lax
from jax.experimental import pallas as pl
from jax.experimental.pallas import tpu as pltpu

F32 = jnp.float32
BF16 = jnp.bfloat16

HEAD_DIM = 128
GROUP = 4
GRID_W = 64
ROPE_THETA = 10000.0
EPS = 1e-6
SUB = 32
HG_BLOCK = 4 * SUB
LANES = 128
SUBLANES = 8
MXU_N = 256
LOG2E = 1.4426950408889634
PIECE_ROWS = 256
VMEM_LIMIT = 56 * 1024 * 1024


def _params(sem):
    return pltpu.CompilerParams(dimension_semantics=sem, vmem_limit_bytes=VMEM_LIMIT)


def _sigmoid(x):
    return 1.0 / (1.0 + jnp.exp(-x))


def _dot(a, b):
    return jnp.dot(a, b, preferred_element_type=F32)


def _dot_nt(a, b):
    return lax.dot_general(a, b, (((1,), (1,)), ((), ())), preferred_element_type=F32)


def _dot_tn(a, b):
    return lax.dot_general(a, b, (((0,), (0,)), ((), ())), preferred_element_type=F32)


def _split_bf16(x):
    hi = x.astype(BF16)
    lo = (x - hi.astype(F32)).astype(BF16)
    return hi, lo


def _ada_kernel(s_ref, w_ref, b_ref, o_ref):
    s = s_ref[...]
    s = s * _sigmoid(s)
    s_hi, s_lo = _split_bf16(s)
    w_hi, w_lo = _split_bf16(w_ref[...])
    rows = s.shape[0]
    r = _dot(jnp.concatenate([s_hi, s_lo], axis=0), w_hi)
    o_ref[...] = r[:rows] + r[rows:] + _dot(s_hi, w_lo) + b_ref[...]


def _ada(s, w, b, tn):
    rows, d = s.shape
    n = w.shape[1]
    return pl.pallas_call(
        _ada_kernel,
        grid=(n // tn,),
        in_specs=[pl.BlockSpec((rows, d), lambda j: (0, 0)),
                  pl.BlockSpec((d, tn), lambda j: (0, j)),
                  pl.BlockSpec((1, tn), lambda j: (0, j))],
        out_specs=pl.BlockSpec((rows, tn), lambda j: (0, j)),
        out_shape=jax.ShapeDtypeStruct((rows, n), F32),
        compiler_params=_params(("arbitrary",)),
        name="ada",
    )(s, w, b)


def _prenorm_kernel(x_ref, c_ref, nw_ref, mod_ref, h_ref, *, n_lat, d):
    def emit(src):
        x = src[...]
        y = x * lax.rsqrt(jnp.mean(x * x, axis=-1, keepdims=True) + EPS) * nw_ref[...]
        shift = mod_ref[0, :, 0:d]
        scale = mod_ref[0, :, d:2 * d]
        h_ref[...] = (y * (1.0 + scale) + shift).astype(BF16)

    i = pl.program_id(0)
    pl.when(i < n_lat)(lambda: emit(x_ref))
    pl.when(i >= n_lat)(lambda: emit(c_ref))


def _prenorm(x2, c2, nw, mod3, batch, tm):
    m_lat, d = x2.shape
    m_ctx = c2.shape[0]
    n_lat, n_ctx = m_lat // tm, m_ctx // tm
    per_b = n_lat // batch
    return pl.pallas_call(
        functools.partial(_prenorm_kernel, n_lat=n_lat, d=d),
        grid=(n_lat + n_ctx,),
        in_specs=[pl.BlockSpec((tm, d), lambda i: (jnp.minimum(i, n_lat - 1), 0)),
                  pl.BlockSpec((tm, d), lambda i: (jnp.maximum(i - n_lat, 0), 0)),
                  pl.BlockSpec((1, d), lambda i: (0, 0)),
                  pl.BlockSpec((1, 1, 3 * d), lambda i: (jnp.minimum(i // per_b, batch), 0, 0))],
        out_specs=pl.BlockSpec((tm, d), lambda i: (i, 0)),
        out_shape=jax.ShapeDtypeStruct((m_lat + m_ctx, d), BF16),
        compiler_params=_params(("arbitrary",)),
        name="prenorm",
    )(x2, c2, nw, mod3)


def _ep_plain(acc, rs, cs, n, extras, outs):
    outs[0][rs, cs] = acc.astype(BF16)


def _ep_gate(acc, rs, cs, n, extras, outs, *, n_silu):
    sg = _sigmoid(acc)
    outs[0][rs, cs] = (sg * jnp.where(n < n_silu, acc, 1.0)).astype(BF16)


def _ep_normrope(acc, rs, cs, n, extras, outs, *, post_scale):
    cos_ref, sin_ref, perm_ref = extras
    cos = cos_ref[rs, :]
    sin = sin_ref[rs, :]
    a_hi, a_lo = _split_bf16(acc)
    partner = _dot(jnp.concatenate([a_hi, a_lo], axis=1), perm_ref[...])
    for hh in range(acc.shape[1] // HEAD_DIM):
        hs = slice(hh * HEAD_DIM, (hh + 1) * HEAD_DIM)
        a = acc[:, hs]
        inv = lax.rsqrt(jnp.mean(a * a, axis=-1, keepdims=True) + EPS) * post_scale
        o = (a * cos + partner[:, hs] * sin) * inv
        outs[0][rs, cs.start + hh * HEAD_DIM:cs.start + (hh + 1) * HEAD_DIM] = o.astype(BF16)


def _block_cumsum(g, reverse):
    r, w = g.shape
    sub = lax.broadcasted_iota(jnp.int32, (SUBLANES, w), 0)
    out = []
    for b0 in range(0, r, HG_BLOCK):
        carry, pre = None, []
        for r0 in range(b0, b0 + HG_BLOCK, SUBLANES):
            x = g[r0:r0 + SUBLANES, :]
            for sh in (1, 2, 4):
                x = x + jnp.where(sub >= sh, pltpu.roll(x, sh, 0), 0.0)
            if carry is not None:
                x = x + carry
            carry = x[SUBLANES - 1:SUBLANES, :]
            pre.append(x)
        pre = jnp.concatenate(pre, axis=0)
        out.append(jnp.where(reverse, carry - pre + g[b0:b0 + HG_BLOCK, :], pre))
    return jnp.concatenate(out, axis=0)


def _ep_forget(acc, rs, cs, n, extras, outs, *, n_fwd):
    lg = extras[0][:, cs]
    mx = jnp.max(lg, axis=0, keepdims=True)
    e = jnp.exp(lg - mx)
    lb = e[0:1, :] / jnp.sum(e, axis=0, keepdims=True)
    f = lb + (1.0 - lb) * _sigmoid(acc)
    outs[0][rs, cs] = _block_cumsum(jnp.log(f), n >= n_fwd)
    outs[1][rs, cs] = (1.0 - f).astype(BF16)


def _inproj_kernel(h_ref, w_hbm, *rest, epilogue, n_extra, n_out, col0, tn, chunk, per_step, n_tiles, m_tiles,
                   piece_rows):
    extras = rest[:n_extra]
    outs = rest[n_extra:n_extra + n_out]
    wbf, stage, acc_scr, sem = rest[n_extra + n_out:]
    t = pl.program_id(0)
    steps = n_tiles * m_tiles
    n, m = t // m_tiles, t % m_tiles
    n_prev = jnp.maximum(t - 1, 0) // m_tiles
    k = wbf.shape[1]
    n_chunks = k // chunk

    def chunk_copy(tile, c, s):
        rows = pl.ds(pl.multiple_of(c * chunk, chunk), chunk)
        cols = pl.ds(pl.multiple_of(col0 + tile * tn, LANES), tn)
        return pltpu.make_async_copy(w_hbm.at[rows, cols], stage.at[s], sem.at[s])

    def land(tile, c, s):
        chunk_copy(tile, c, s).wait()
        rows = pl.ds(pl.multiple_of(c * chunk, chunk), chunk)
        wbf[tile % 2, rows, :] = stage[s].astype(BF16)

    @pl.when(t == 0)
    def _():
        acc_scr[...] = jnp.zeros_like(acc_scr)
        chunk_copy(0, 0, 0).start()
        for c in range(n_chunks):
            if c + 1 < n_chunks:
                chunk_copy(0, c + 1, (c + 1) % 2).start()
            land(0, c, c % 2)

    pm = jnp.where(m == 0, m_tiles - 1, m - 1)
    ptile = jnp.where(m == 0, n, n + 1)
    for i in range(per_step):
        c = pm * per_step + i

        @pl.when((t > 0) & (c < n_chunks) & (ptile < n_tiles))
        def _(c=c, i=i):
            land(ptile, c, i)

    for i in range(per_step):
        c = m * per_step + i

        @pl.when((c < n_chunks) & (n + 1 < n_tiles))
        def _(c=c, i=i):
            chunk_copy(n + 1, c, i).start()

    sw = min(MXU_N, tn)
    slices = [slice(j * sw, (j + 1) * sw) for j in range(tn // sw)]
    tm = h_ref.shape[0]
    rp = min(piece_rows, tm)
    rows = [slice(r0, r0 + rp) for r0 in range(0, tm, rp)]

    @pl.when(t < steps)
    def _():
        slot = n % 2
        for rs in rows:
            for cs in slices:
                epilogue(acc_scr[rs, cs], rs, cs, n_prev, extras, outs)
            acc_scr[rs, :] = _dot(h_ref[rs, :], wbf[slot])

    @pl.when(t == steps)
    def _():
        for cs in slices:
            for rs in rows:
                epilogue(acc_scr[rs, cs], rs, cs, n_prev, extras, outs)


def _inproj(h, w, col0, width, rows, epilogue, extras, extra_specs, out_dtypes, tm, tn, name):
    k = h.shape[1]
    n_tiles, m_tiles = width // tn, rows // tm
    steps = n_tiles * m_tiles
    chunk = min(256, k // 2)
    n_chunks = k // chunk
    per_step = -(-n_chunks // m_tiles)
    n_stage = max(2, per_step)

    def prev_tile(t):
        tp = jnp.maximum(t - 1, 0)
        return tp // m_tiles, tp % m_tiles

    def deferred(fn):
        return lambda t: fn(*prev_tile(t))

    in_specs = [pl.BlockSpec((tm, k), lambda t: (jnp.minimum(t, steps - 1) % m_tiles, 0)),
                pl.BlockSpec(memory_space=pl.ANY)]
    in_specs += [pl.BlockSpec(shape, deferred(fn)) for shape, fn in extra_specs]
    out_specs = [pl.BlockSpec((tm, tn), deferred(lambda n, m: (m, n))) for _ in out_dtypes]
    out_shape = [jax.ShapeDtypeStruct((rows, width), dt) for dt in out_dtypes]
    return pl.pallas_call(
        functools.partial(_inproj_kernel, epilogue=epilogue, n_extra=len(extras), n_out=len(out_dtypes),
                          col0=col0, tn=tn, chunk=chunk, per_step=per_step, n_tiles=n_tiles, m_tiles=m_tiles,
                          piece_rows=PIECE_ROWS),
        grid=(steps + 1,),
        in_specs=in_specs,
        out_specs=out_specs,
        out_shape=out_shape,
        scratch_shapes=[pltpu.VMEM((2, k, tn), BF16),
                        pltpu.VMEM((n_stage, chunk, tn), F32),
                        pltpu.VMEM((tm, tn), F32),
                        pltpu.SemaphoreType.DMA((n_stage,))],
        compiler_params=_params(("arbitrary",)),
        name=name,
    )(h, w, *extras)


def _col_reduce(x, op):
    r = x.shape[0]
    while r % (2 * SUBLANES) == 0 and r > SUBLANES:
        r //= 2
        x = op(x[:r], x[r:])
    return x


def _attn_kernel(q_ref, kl_ref, vl_ref, kc_ref, vc_ref, sg_ref, *rest, key_chunk, n_side):
    side_in = rest[:n_side]
    o_ref = rest[n_side]
    side_out = rest[n_side + 1:2 * n_side + 1]
    k_scr, vt_scr, st_in, st_out, sem_in, sem_out = rest[2 * n_side + 1:]
    t_len = kl_ref.shape[0]
    n_keys = k_scr.shape[0]
    slab = st_in.shape[1]
    step = (pl.program_id(0) * pl.num_programs(1) + pl.program_id(1)) * pl.num_programs(2) + pl.program_id(2)
    last_step = pl.num_programs(0) * pl.num_programs(1) * pl.num_programs(2) - 1

    def slab_in(w, at_step):
        rows = pl.ds(pl.multiple_of(at_step * slab, slab), slab)
        return pltpu.make_async_copy(side_in[w].at[rows, :], st_in.at[w], sem_in.at[w])

    def slab_out(w, at_step):
        rows = pl.ds(pl.multiple_of(at_step * slab, slab), slab)
        return pltpu.make_async_copy(st_out.at[w], side_out[w].at[rows, :], sem_out.at[w])

    for w in range(n_side):
        slab_in(w, step).start()

    @pl.when(pl.program_id(2) == 0)
    def _():
        k_scr[0:t_len, :] = kl_ref[...]
        k_scr[t_len:, :] = kc_ref[...]
        vt_scr[0:HEAD_DIM, 0:t_len] = vl_ref[...].astype(F32).T.astype(BF16)
        vt_scr[0:HEAD_DIM, t_len:] = vc_ref[...].astype(F32).T.astype(BF16)
        vt_scr[HEAD_DIM:, :] = jnp.ones((vt_scr.shape[0] - HEAD_DIM, vt_scr.shape[1]), BF16)

    heads = [slice(g * HEAD_DIM, (g + 1) * HEAD_DIM) for g in range(GROUP)]
    chunks = [slice(c * key_chunk, (c + 1) * key_chunk) for c in range(n_keys // key_chunk)]

    def scores(g, ks):
        return _dot_nt(k_scr[ks, :], q_ref[:, heads[g]])

    def weigh(s_chunk, mx, ks):
        p = jnp.exp2(s_chunk - mx).astype(BF16)
        return _dot(vt_scr[:, ks], p)

    def col_max(s_chunks):
        part = [_col_reduce(sc, jnp.maximum) for sc in s_chunks]
        return jnp.max(functools.reduce(jnp.maximum, part), axis=0, keepdims=True)

    s_cur = [scores(0, ks) for ks in chunks]
    for g in range(GROUP):
        mx = col_max(s_cur)
        s_nxt, acc = [], None
        for c, ks in enumerate(chunks):
            if g + 1 < GROUP:
                s_nxt.append(scores(g + 1, ks))
            pv = weigh(s_cur[c], mx, ks)
            acc = pv if acc is None else acc + pv
        o = acc[0:HEAD_DIM] / acc[HEAD_DIM:HEAD_DIM + 1]
        o_ref[:, heads[g]] = (o.T * sg_ref[:, heads[g]].astype(F32)).astype(BF16)
        s_cur = s_nxt

    for w in range(n_side):
        slab_in(w, step).wait()

        @pl.when(step > 0)
        def _(w=w):
            slab_out(w, step - 1).wait()

        st_out[w] = st_in[w].astype(BF16)
        slab_out(w, step).start()

        @pl.when(step == last_step)
        def _(w=w):
            slab_out(w, step).wait()


def _attention(q, kk, vv, sg, batch, t_len, c_len, tq, side):
    m_lat = batch * t_len
    n_kv = kk.shape[1] // HEAD_DIM
    gw = GROUP * HEAD_DIM
    nq = t_len // tq
    ctx0 = m_lat // c_len
    n_keys = t_len + c_len
    half = n_keys // 2
    key_chunk = half if half % LANES == 0 else n_keys
    steps = batch * n_kv * nq
    rows, cols = side[0].shape
    assert all(w.shape == (rows, cols) for w in side) and rows % (steps * 2 * SUBLANES) == 0
    slab = rows // steps
    any_spec = pl.BlockSpec(memory_space=pl.ANY)
    return pl.pallas_call(
        functools.partial(_attn_kernel, key_chunk=key_chunk, n_side=len(side)),
        grid=(batch, n_kv, nq),
        in_specs=[pl.BlockSpec((tq, gw), lambda b, h, i: (b * nq + i, h)),
                  pl.BlockSpec((t_len, HEAD_DIM), lambda b, h, i: (b, h)),
                  pl.BlockSpec((t_len, HEAD_DIM), lambda b, h, i: (b, h)),
                  pl.BlockSpec((c_len, HEAD_DIM), lambda b, h, i: (ctx0 + b, h)),
                  pl.BlockSpec((c_len, HEAD_DIM), lambda b, h, i: (ctx0 + b, h)),
                  pl.BlockSpec((tq, gw), lambda b, h, i: (b * nq + i, h))] + [any_spec] * len(side),
        out_specs=[pl.BlockSpec((tq, gw), lambda b, h, i: (b * nq + i, h))] + [any_spec] * len(side),
        out_shape=[jax.ShapeDtypeStruct((m_lat, n_kv * gw), BF16)]
                  + [jax.ShapeDtypeStruct((rows, cols), BF16) for _ in side],
        scratch_shapes=[pltpu.VMEM((n_keys, HEAD_DIM), BF16),
                        pltpu.VMEM((HEAD_DIM + 2 * SUBLANES, n_keys), BF16),
                        pltpu.VMEM((len(side), slab, cols), F32),
                        pltpu.VMEM((len(side), slab, cols), BF16),
                        pltpu.SemaphoreType.DMA((len(side),)),
                        pltpu.SemaphoreType.DMA((len(side),))],
        compiler_params=_params(("arbitrary", "arbitrary", "arbitrary")),
        name="attn",
    )(q, kk, vv, kk, vv, sg, *side)


def _hgrn_masks(c, reverse):
    ns = c // SUB
    row = lax.broadcasted_iota(jnp.int32, (c, c), 0)
    col = lax.broadcasted_iota(jnp.int32, (c, c), 1)
    pr, pc = row // SUB, col // SUB
    if reverse:
        pr, pc = ns - 1 - pr, ns - 1 - pc
        causal = col >= row
    else:
        causal = col <= row
    m_diag = (pr == pc) & causal
    m_adj = ((pr == 1) & (pc == 0)) | ((pr == 3) & (pc == 2))
    m_far = (pr >= 2) & (pc <= 1)
    return m_diag, m_adj, m_far


def _hgrn_prepare(q, k, bfull, reverse, want_out):
    c = k.shape[0]
    ns = c // SUB
    assert ns == 4
    order = list(range(ns))[::-1] if reverse else list(range(ns))
    zero = jnp.zeros((1, LANES), F32)
    start, end = {}, {}
    for i in range(ns):
        lo, hi = i * SUB, i * SUB + SUB - 1
        if reverse:
            start[i] = bfull[hi + 1:hi + 2, :] if i < ns - 1 else zero
            end[i] = bfull[lo:lo + 1, :]
        else:
            start[i] = bfull[lo - 1:lo, :] if i > 0 else zero
            end[i] = bfull[hi:hi + 1, :]
    btot = end[order[-1]]
    qt, khat, kd, q_far, k_far, qs, ke = {}, {}, {}, {}, {}, {}, {}
    for p, i in enumerate(order):
        rs = slice(i * SUB, (i + 1) * SUB)
        b_i = bfull[rs]
        kd[i] = k[rs] * jnp.exp(end[i] - b_i)
        ke[i] = kd[i] if p == ns - 1 else kd[i] * jnp.exp(btot - end[i])
        if want_out:
            bl = b_i - start[i]
            qt[i] = q[rs] * jnp.exp(bl)
            khat[i] = k[rs] * jnp.exp(-bl)
            qs[i] = qt[i] if p == 0 else qt[i] * jnp.exp(start[i])
            q_far[i] = qt[i] * jnp.exp(start[i] - start[order[2]]) if p == 3 else qt[i]
            k_far[i] = kd[i] * jnp.exp(start[order[2]] - end[i]) if p == 0 else kd[i]

    def cat(d):
        return jnp.concatenate([d[i].astype(BF16) for i in range(ns)], axis=0)

    out = {"ke": cat(ke), "dec": jnp.exp(btot)}
    if want_out:
        out.update(qt=cat(qt), kk=jnp.concatenate([cat(kd), cat(khat)], axis=0),
                   q_far=cat(q_far), k_far=cat(k_far), qs=cat(qs))
    return out


def _hgrn_kernel(*refs, reverse, n_ctx, heads, final):
    if final:
        q_ref, k_ref, b_ref, v_ref, of_ref, sg_ref, hw_ref, o_ref, st_ref = refs
    else:
        q_ref, k_ref, b_ref, v_ref, o_ref, st_ref = refs
    s = pl.program_id(2)

    @pl.when(s == 0)
    def _():
        st_ref[...] = jnp.zeros_like(st_ref)

    c = HG_BLOCK
    n_sub = k_ref.shape[0] // c
    order = list(range(n_sub))[::-1] if reverse else list(range(n_sub))
    cols = [slice(j * LANES, (j + 1) * LANES) for j in range(heads)]

    def run(want_out):
        if want_out:
            m_diag, m_adj, m_far = _hgrn_masks(c, reverse)
        for bi in order:
            rows = slice(bi * c, (bi + 1) * c)
            prep = [_hgrn_prepare(q_ref[rows, cs].astype(F32) if want_out else None,
                                  k_ref[rows, cs].astype(F32), b_ref[rows, cs], reverse, want_out)
                    for cs in cols]
            if want_out:
                near = [_dot_nt(p["qt"], p["kk"]) for p in prep]
                far = [_dot_nt(p["q_far"], p["k_far"]) for p in prep]
                inter = [_dot_nt(p["qs"], st_ref[j].astype(BF16)) for j, p in enumerate(prep)]
            for j, (cs, p) in enumerate(zip(cols, prep)):
                st_ref[j] = p["dec"] * st_ref[j] + _dot_tn(v_ref[rows, cs], p["ke"])
            if want_out:
                amat = [jnp.where(m_diag, n[:, c:], jnp.where(m_adj, n[:, :c], jnp.where(m_far, f, 0.0))).astype(BF16)
                        for n, f in zip(near, far)]
                for j, cs in enumerate(cols):
                    o = _dot(amat[j], v_ref[rows, cs]) + inter[j]
                    if final:
                        tot = o + of_ref[rows, cs]
                        y = tot * lax.rsqrt(jnp.mean(tot * tot, axis=-1, keepdims=True) + EPS)
                        y = y * hw_ref[:, cs]
                        o_ref[rows, cs] = (y * sg_ref[rows, cs].astype(F32)).astype(o_ref.dtype)
                    else:
                        o_ref[rows, cs] = o.astype(o_ref.dtype)

    pl.when(s < n_ctx)(lambda: run(False))
    pl.when(s >= n_ctx)(lambda: run(True))


def _hgrn(q, kk, bb, vv, w, batch, t_len, c_len, reverse, heads, cols, blk, final_args=None):
    m_lat = batch * t_len
    hb = heads * LANES
    n_ctx, n_lat = c_len // blk, t_len // blk
    ctx0 = m_lat // blk
    q0, k0, v0, sg0 = (cc // hb for cc in cols)

    def row_block(b, s):
        if reverse:
            cblk = ctx0 + b * n_ctx + (n_ctx - 1 - s)
            lblk = b * n_lat + (n_lat - 1 - (s - n_ctx))
        else:
            cblk = ctx0 + b * n_ctx + s
            lblk = b * n_lat + (s - n_ctx)
        return jnp.where(s < n_ctx, cblk, lblk)

    def lat_block(b, s):
        sl = jnp.maximum(s - n_ctx, 0)
        return b * n_lat + ((n_lat - 1 - sl) if reverse else sl)

    def all_spec(c0):
        return pl.BlockSpec((blk, hb), lambda b, h, s: (row_block(b, s), c0 + h))

    def lat_spec(c0):
        return pl.BlockSpec((blk, hb), lambda b, h, s: (lat_block(b, s), c0 + h))

    in_specs = [lat_spec(q0), all_spec(k0), all_spec(k0), all_spec(v0)]
    args = [q, kk, bb, vv]
    final = final_args is not None
    if final:
        o_f, sg, hw = final_args
        in_specs += [lat_spec(0), lat_spec(sg0), pl.BlockSpec((1, hb), lambda b, h, s: (0, h))]
        args += [o_f, sg, hw]
    return pl.pallas_call(
        functools.partial(_hgrn_kernel, reverse=reverse, n_ctx=n_ctx, heads=heads, final=final),
        grid=(batch, w // hb, n_ctx + n_lat),
        in_specs=in_specs,
        out_specs=lat_spec(0),
        out_shape=jax.ShapeDtypeStruct((m_lat, w), BF16 if final else F32),
        scratch_shapes=[pltpu.VMEM((heads, LANES, LANES), F32)],
        compiler_params=_params(("arbitrary", "arbitrary", "arbitrary")),
        name="hgrn_bwd" if reverse else "hgrn_fwd",
    )(*args)


def _proj_kernel(yh_ref, ya_ref, wh_ref, wa_ref, rh_ref, ra_ref, o_ref):
    a = _dot(yh_ref[...], wh_ref[...])
    b = _dot(ya_ref[...], wa_ref[...])
    o_ref[...] = (rh_ref[...].astype(F32) * a + ra_ref[...].astype(F32) * b).astype(BF16)


def _proj(yh, ya, wh, wa, gates, rh_col0, ra_col0, tm, tn):
    m, k = yh.shape
    n = wh.shape[1]
    rh0, ra0 = rh_col0 // tn, ra_col0 // tn
    return pl.pallas_call(
        _proj_kernel,
        grid=(m // tm, n // tn),
        in_specs=[pl.BlockSpec((tm, k), lambda i, j: (i, 0)),
                  pl.BlockSpec((tm, k), lambda i, j: (i, 0)),
                  pl.BlockSpec((k, tn), lambda i, j: (0, j)),
                  pl.BlockSpec((k, tn), lambda i, j: (0, j)),
                  pl.BlockSpec((tm, tn), lambda i, j: (i, rh0 + j)),
                  pl.BlockSpec((tm, tn), lambda i, j: (i, ra0 + j))],
        out_specs=pl.BlockSpec((tm, tn), lambda i, j: (i, j)),
        out_shape=jax.ShapeDtypeStruct((m, n), BF16),
        compiler_params=_params(("arbitrary", "arbitrary")),
        name="proj",
    )(yh, ya, wh, wa, gates, gates)


def _out_kernel(mg_ref, w_ref, x_ref, gate_ref, fw_ref, o_ref, xn_scr):
    n = pl.program_id(1)
    n_tiles, _, tn = xn_scr.shape
    xn_scr[n] = x_ref[...] + gate_ref[0] * _dot(mg_ref[...], w_ref[...])

    @pl.when(n == n_tiles - 1)
    def _():
        ssq = None
        for j in range(n_tiles):
            v = xn_scr[j]
            part = jnp.sum(v * v, axis=-1, keepdims=True)
            ssq = part if ssq is None else ssq + part
        inv = lax.rsqrt(ssq / (n_tiles * tn) + EPS)
        for j in range(n_tiles):
            cs = slice(j * tn, (j + 1) * tn)
            o_ref[:, cs] = xn_scr[j] * inv * fw_ref[:, cs]


def _outproj(mg, w, x2, mod3, fw, t_len, tm, tn):
    m, k = mg.shape
    n = w.shape[1]
    gate0 = 2 * n // tn
    per_b = t_len // tm
    return pl.pallas_call(
        _out_kernel,
        grid=(m // tm, n // tn),
        in_specs=[pl.BlockSpec((tm, k), lambda i, j: (i, 0)),
                  pl.BlockSpec((k, tn), lambda i, j: (0, j)),
                  pl.BlockSpec((tm, tn), lambda i, j: (i, j)),
                  pl.BlockSpec((1, 1, tn), lambda i, j: (i // per_b, 0, gate0 + j)),
                  pl.BlockSpec((1, n), lambda i, j: (0, 0))],
        out_specs=pl.BlockSpec((tm, n), lambda i, j: (i, 0)),
        out_shape=jax.ShapeDtypeStruct((m, n), F32),
        scratch_shapes=[pltpu.VMEM((n // tn, tm, tn), F32)],
        compiler_params=_params(("arbitrary", "arbitrary")),
        name="outproj",
    )(mg, w, x2, mod3, fw)


def _rope_tables(t_len, pad_rows):
    axis_dim = HEAD_DIM // 2
    rows = t_len // GRID_W
    row = jnp.repeat(jnp.arange(rows, dtype=F32), GRID_W)
    col = jnp.tile(jnp.arange(GRID_W, dtype=F32), rows)
    inv = ROPE_THETA ** (-jnp.arange(0, axis_dim, 2, dtype=F32) / axis_dim)
    fr = row[:, None] * inv[None]
    fc = col[:, None] * inv[None]
    ang = jnp.concatenate([fr, fr, fc, fc], axis=-1)
    cos, sin = jnp.cos(ang), jnp.sin(ang)
    lane = jnp.arange(HEAD_DIM)
    sign = jnp.where((lane // (HEAD_DIM // 4)) % 2 == 0, -1.0, 1.0).astype(F32)
    cos = jnp.concatenate([cos, jnp.ones((pad_rows, HEAD_DIM), F32)], axis=0)
    sin = jnp.concatenate([sin * sign, jnp.zeros((pad_rows, HEAD_DIM), F32)], axis=0)
    return cos, sin


def _tile(n, pref):
    t = min(n, pref)
    while n % t:
        t //= 2
    return t


def kernel(x, c, ctx, c_ctx, norm_w, w_ada, b_ada, w_in, q_norm_w, k_norm_w, lb_logits,
           hg_norm_w, w_proj_hg, w_proj_att, w_out, final_norm_w):
    batch, t_len, d = x.shape
    c_len = ctx.shape[1]
    depth = norm_w.shape[0]
    assert depth == 1, "single-layer block"
    assert t_len % HG_BLOCK == 0 and c_len % HG_BLOCK == 0
    att_w = d
    kv_w = d // GROUP
    m_lat, m_ctx = batch * t_len, batch * c_len
    m_all = m_lat + m_ctx

    x2 = x.reshape(m_lat, d)
    c2 = ctx.reshape(m_ctx, d)

    pad = (-(batch + 1)) % SUBLANES
    s = jnp.concatenate([c, c_ctx[None, :], jnp.zeros((pad, d), F32)], axis=0)
    mod = _ada(s, w_ada[0], b_ada[0][None, :], _tile(3 * d, 512))
    mod3 = mod.reshape(mod.shape[0], 1, 3 * d)

    h = _prenorm(x2, c2, norm_w[0][None, :], mod3, batch, _tile(math.gcd(m_ctx, t_len), 512))

    tm = _tile(m_ctx, 1024)
    tn = _tile(kv_w, 1024)
    n_lat_tiles = m_lat // tm
    per_seq = t_len // tm
    cos, sin = _rope_tables(t_len, tm)
    sw = min(MXU_N, tn)
    lane = jnp.arange(sw)
    perm = (lane[:, None] == (lane[None, :] ^ (HEAD_DIM // 4))).astype(BF16)
    perm2 = jnp.concatenate([perm, perm], axis=0)
    partner = jnp.arange(HEAD_DIM) ^ (HEAD_DIM // 4)

    def rope_inputs(nw):
        return [cos * nw[None, :], sin * nw[partner][None, :], perm2]

    def rope_row(n, m):
        return (jnp.where(m < n_lat_tiles, m % per_seq, per_seq), 0)

    rope_specs = [((tm, HEAD_DIM), rope_row), ((tm, HEAD_DIM), rope_row), ((2 * sw, sw), lambda n, m: (0, 0))]
    w = w_in[0]
    c_k, c_v, c_f, c_q, c_g = 0, kv_w, 2 * kv_w + d, 2 * kv_w + 3 * d, 2 * kv_w + 3 * d + att_w

    (k_att,) = _inproj(h, w, c_k, kv_w, m_all, functools.partial(_ep_normrope, post_scale=1.0),
                       rope_inputs(k_norm_w[0]), rope_specs, [BF16], tm, tn, "inproj_k")
    (v_att,) = _inproj(h, w, c_v, kv_w, m_all, _ep_plain, [], [], [BF16], tm, tn, "inproj_v")
    (i_hg,) = _inproj(h, w, c_v + kv_w, d, m_all, _ep_plain, [], [], [BF16], tm, tn, "inproj_i")
    slots = lb_logits.shape[1]
    lb2 = jnp.transpose(lb_logits, (1, 0, 2)).reshape(slots, 2 * d)
    b_all, k_all = _inproj(h, w, c_f, 2 * d, m_all, functools.partial(_ep_forget, n_fwd=d // tn),
                           [lb2], [((slots, tn), lambda n, m: (0, n))], [F32, BF16], tm, tn, "inproj_f")
    (q_att,) = _inproj(h, w, c_q, att_w, m_lat,
                       functools.partial(_ep_normrope, post_scale=HEAD_DIM ** -0.5 * LOG2E),
                       rope_inputs(q_norm_w[0]), rope_specs, [BF16], tm, tn, "inproj_q")
    (gates,) = _inproj(h, w, c_g, att_w + 4 * d, m_lat,
                       functools.partial(_ep_gate, n_silu=(att_w + 2 * d) // tn),
                       [], [], [BF16], tm, tn, "inproj_g")
    g_qh, g_gh, g_rh, g_ra = att_w, att_w + d, att_w + 2 * d, att_w + 3 * d

    y_att, wh, wa, wo = _attention(q_att, k_att, v_att, gates, batch, t_len, c_len, _tile(t_len, 1024),
                                   [w_proj_hg[0], w_proj_att[0], w_out[0]])

    heads = _tile(d // LANES, 32)
    blk = _tile(c_len, 2 * HG_BLOCK)
    o_f = _hgrn(gates, k_all, b_all, i_hg, d, batch, t_len, c_len, False, heads, (g_qh, 0, 0, 0), blk)
    y_hg = _hgrn(gates, k_all, b_all, i_hg, d, batch, t_len, c_len, True, heads, (g_qh, d, 0, g_gh), blk,
                 final_args=(o_f, gates, hg_norm_w[0][None, :]))

    tm2 = _tile(t_len, 512)
    tn2 = _tile(d, 512)
    merged = _proj(y_hg, y_att, wh, wa, gates, g_rh, g_ra, tm2, tn2)
    out = _outproj(merged, wo, x2, mod3, final_norm_w[None, :], t_len, tm2, tn2)
    return out.reshape(batch, t_len, d)
```

```python
import functools
import math

import jax
import jax.numpy as jnp
from jax import lax
from jax.experimental import pallas as pl
from jax.experimental.pallas import tpu as pltpu

F32 = jnp.float32
BF16 = jnp.bfloat16

HEAD_DIM = 128
GROUP = 4
GRID_W = 64
ROPE_THETA = 10000.0
EPS = 1e-6
SUB = 32
HG_BLOCK = 4 * SUB
LANES = 128
SUBLANES = 8
MXU_N = 256
LOG2E = 1.4426950408889634
PIECE_ROWS = 256
VMEM_LIMIT = 56 * 1024 * 1024


def _params(sem):
    return pltpu.CompilerParams(dimension_semantics=sem, vmem_limit_bytes=VMEM_LIMIT)


def _sigmoid(x):
    return 1.0 / (1.0 + jnp.exp(-x))


def _dot(a, b):
    return jnp.dot(a, b, preferred_element_type=F32)


def _dot_nt(a, b):
    return lax.dot_general(a, b, (((1,), (1,)), ((), ())), preferred_element_type=F32)


def _dot_tn(a, b):
    return lax.dot_general(a, b, (((0,), (0,)), ((), ())), preferred_element_type=F32)


def _split_bf16(x):
    hi = x.astype(BF16)
    lo = (x - hi.astype(F32)).astype(BF16)
    return hi, lo


def _ada_kernel(s_ref, w_ref, b_ref, o_ref):
    s = s_ref[...]
    s = s * _sigmoid(s)
    s_hi, s_lo = _split_bf16(s)
    w_hi, w_lo = _split_bf16(w_ref[...])
    rows = s.shape[0]
    r = _dot(jnp.concatenate([s_hi, s_lo], axis=0), w_hi)
    o_ref[...] = r[:rows] + r[rows:] + _dot(s_hi, w_lo) + b_ref[...]


def _ada(s, w, b, tn):
    rows, d = s.shape
    n = w.shape[1]
    return pl.pallas_call(
        _ada_kernel,
        grid=(n // tn,),
        in_specs=[pl.BlockSpec((rows, d), lambda j: (0, 0)),
                  pl.BlockSpec((d, tn), lambda j: (0, j)),
                  pl.BlockSpec((1, tn), lambda j: (0, j))],
        out_specs=pl.BlockSpec((rows, tn), lambda j: (0, j)),
        out_shape=jax.ShapeDtypeStruct((rows, n), F32),
        compiler_params=_params(("arbitrary",)),
        name="ada",
    )(s, w, b)


def _prenorm_kernel(x_ref, c_ref, nw_ref, mod_ref, h_ref, *, n_lat, d):
    def emit(src):
        x = src[...]
        y = x * lax.rsqrt(jnp.mean(x * x, axis=-1, keepdims=True) + EPS) * nw_ref[...]
        shift = mod_ref[0, :, 0:d]
        scale = mod_ref[0, :, d:2 * d]
        h_ref[...] = (y * (1.0 + scale) + shift).astype(BF16)

    i = pl.program_id(0)
    pl.when(i < n_lat)(lambda: emit(x_ref))
    pl.when(i >= n_lat)(lambda: emit(c_ref))


def _prenorm(x2, c2, nw, mod3, batch, tm):
    m_lat, d = x2.shape
    m_ctx = c2.shape[0]
    n_lat, n_ctx = m_lat // tm, m_ctx // tm
    per_b = n_lat // batch
    return pl.pallas_call(
        functools.partial(_prenorm_kernel, n_lat=n_lat, d=d),
        grid=(n_lat + n_ctx,),
        in_specs=[pl.BlockSpec((tm, d), lambda i: (jnp.minimum(i, n_lat - 1), 0)),
                  pl.BlockSpec((tm, d), lambda i: (jnp.maximum(i - n_lat, 0), 0)),
                  pl.BlockSpec((1, d), lambda i: (0, 0)),
                  pl.BlockSpec((1, 1, 3 * d), lambda i: (jnp.minimum(i // per_b, batch), 0, 0))],
        out_specs=pl.BlockSpec((tm, d), lambda i: (i, 0)),
        out_shape=jax.ShapeDtypeStruct((m_lat + m_ctx, d), BF16),
        compiler_params=_params(("arbitrary",)),
        name="prenorm",
    )(x2, c2, nw, mod3)


def _ep_plain(acc, rs, cs, n, extras, outs):
    outs[0][rs, cs] = acc.astype(BF16)


def _ep_gate(acc, rs, cs, n, extras, outs, *, n_silu):
    sg = _sigmoid(acc)
    outs[0][rs, cs] = (sg * jnp.where(n < n_silu, acc, 1.0)).astype(BF16)


def _ep_normrope(acc, rs, cs, n, extras, outs, *, post_scale):
    cos_ref, sin_ref, perm_ref = extras
    cos = cos_ref[rs, :]
    sin = sin_ref[rs, :]
    partner = _dot(acc.astype(BF16), perm_ref[0:acc.shape[1], :])
    for hh in range(acc.shape[1] // HEAD_DIM):
        hs = slice(hh * HEAD_DIM, (hh + 1) * HEAD_DIM)
        a = acc[:, hs]
        inv = lax.rsqrt(jnp.mean(a * a, axis=-1, keepdims=True) + EPS) * post_scale
        o = (a * cos + partner[:, hs] * sin) * inv
        outs[0][rs, cs.start + hh * HEAD_DIM:cs.start + (hh + 1) * HEAD_DIM] = o.astype(BF16)


def _block_cumsum(g, reverse):
    r, w = g.shape
    sub = lax.broadcasted_iota(jnp.int32, (SUBLANES, w), 0)
    out = []
    for b0 in range(0, r, HG_BLOCK):
        carry, pre = None, []
        for r0 in range(b0, b0 + HG_BLOCK, SUBLANES):
            x = g[r0:r0 + SUBLANES, :]
            for sh in (1, 2, 4):
                x = x + jnp.where(sub >= sh, pltpu.roll(x, sh, 0), 0.0)
            if carry is not None:
                x = x + carry
            carry = x[SUBLANES - 1:SUBLANES, :]
            pre.append(x)
        pre = jnp.concatenate(pre, axis=0)
        out.append(jnp.where(reverse, carry - pre + g[b0:b0 + HG_BLOCK, :], pre))
    return jnp.concatenate(out, axis=0)


def _ep_forget(acc, rs, cs, n, extras, outs, *, n_fwd):
    lg = extras[0][:, cs]
    mx = jnp.max(lg, axis=0, keepdims=True)
    e = jnp.exp(lg - mx)
    lb = e[0:1, :] / jnp.sum(e, axis=0, keepdims=True)
    f = lb + (1.0 - lb) * _sigmoid(acc)
    outs[0][rs, cs] = _block_cumsum(jnp.log(f), n >= n_fwd)
    outs[1][rs, cs] = (1.0 - f).astype(BF16)


def _inproj_kernel(h_ref, w_hbm, *rest, epilogue, n_extra, n_out, col0, tn, chunk, per_step, n_tiles, m_tiles,
                   piece_rows):
    extras = rest[:n_extra]
    outs = rest[n_extra:n_extra + n_out]
    wbf, stage, acc_scr, sem = rest[n_extra + n_out:]
    t = pl.program_id(0)
    steps = n_tiles * m_tiles
    n, m = t // m_tiles, t % m_tiles
    n_prev = jnp.maximum(t - 1, 0) // m_tiles
    k = wbf.shape[1]
    n_chunks = k // chunk

    def chunk_copy(tile, c, s):
        rows = pl.ds(pl.multiple_of(c * chunk, chunk), chunk)
        cols = pl.ds(pl.multiple_of(col0 + tile * tn, LANES), tn)
        return pltpu.make_async_copy(w_hbm.at[rows, cols], stage.at[s], sem.at[s])

    def land(tile, c, s):
        chunk_copy(tile, c, s).wait()
        rows = pl.ds(pl.multiple_of(c * chunk, chunk), chunk)
        wbf[tile % 2, rows, :] = stage[s].astype(BF16)

    @pl.when(t == 0)
    def _():
        acc_scr[...] = jnp.zeros_like(acc_scr)
        chunk_copy(0, 0, 0).start()
        for c in range(n_chunks):
            if c + 1 < n_chunks:
                chunk_copy(0, c + 1, (c + 1) % 2).start()
            land(0, c, c % 2)

    pm = jnp.where(m == 0, m_tiles - 1, m - 1)
    ptile = jnp.where(m == 0, n, n + 1)
    for i in range(per_step):
        c = pm * per_step + i

        @pl.when((t > 0) & (c < n_chunks) & (ptile < n_tiles))
        def _(c=c, i=i):
            land(ptile, c, i)

    for i in range(per_step):
        c = m * per_step + i

        @pl.when((c < n_chunks) & (n + 1 < n_tiles))
        def _(c=c, i=i):
            chunk_copy(n + 1, c, i).start()

    sw = min(MXU_N, tn)
    slices = [slice(j * sw, (j + 1) * sw) for j in range(tn // sw)]
    tm = h_ref.shape[0]
    rp = min(piece_rows, tm)
    rows = [slice(r0, r0 + rp) for r0 in range(0, tm, rp)]

    @pl.when(t < steps)
    def _():
        slot = n % 2
        for rs in rows:
            for cs in slices:
                epilogue(acc_scr[rs, cs], rs, cs, n_prev, extras, outs)
            acc_scr[rs, :] = _dot(h_ref[rs, :], wbf[slot])

    @pl.when(t == steps)
    def _():
        for cs in slices:
            for rs in rows:
                epilogue(acc_scr[rs, cs], rs, cs, n_prev, extras, outs)


def _inproj(h, w, col0, width, rows, epilogue, extras, extra_specs, out_dtypes, tm, tn, name):
    k = h.shape[1]
    n_tiles, m_tiles = width // tn, rows // tm
    steps = n_tiles * m_tiles
    chunk = min(256, k // 2)
    n_chunks = k // chunk
    per_step = -(-n_chunks // m_tiles)
    n_stage = max(2, per_step)

    def prev_tile(t):
        tp = jnp.maximum(t - 1, 0)
        return tp // m_tiles, tp % m_tiles

    def deferred(fn):
        return lambda t: fn(*prev_tile(t))

    in_specs = [pl.BlockSpec((tm, k), lambda t: (jnp.minimum(t, steps - 1) % m_tiles, 0)),
                pl.BlockSpec(memory_space=pl.ANY)]
    in_specs += [pl.BlockSpec(shape, deferred(fn)) for shape, fn in extra_specs]
    out_specs = [pl.BlockSpec((tm, tn), deferred(lambda n, m: (m, n))) for _ in out_dtypes]
    out_shape = [jax.ShapeDtypeStruct((rows, width), dt) for dt in out_dtypes]
    return pl.pallas_call(
        functools.partial(_inproj_kernel, epilogue=epilogue, n_extra=len(extras), n_out=len(out_dtypes),
                          col0=col0, tn=tn, chunk=chunk, per_step=per_step, n_tiles=n_tiles, m_tiles=m_tiles,
                          piece_rows=PIECE_ROWS),
        grid=(steps + 1,),
        in_specs=in_specs,
        out_specs=out_specs,
        out_shape=out_shape,
        scratch_shapes=[pltpu.VMEM((2, k, tn), BF16),
                        pltpu.VMEM((n_stage, chunk, tn), F32),
                        pltpu.VMEM((tm, tn), F32),
                        pltpu.SemaphoreType.DMA((n_stage,))],
        compiler_params=_params(("arbitrary",)),
        name=name,
    )(h, w, *extras)


def _col_reduce(x, op):
    r = x.shape[0]
    while r % (2 * SUBLANES) == 0 and r > SUBLANES:
        r //= 2
        x = op(x[:r], x[r:])
    return x


def _attn_kernel(q_ref, kl_ref, vl_ref, kc_ref, vc_ref, sg_ref, *rest, key_chunk, n_side):
    side_in = rest[:n_side]
    o_ref = rest[n_side]
    side_out = rest[n_side + 1:2 * n_side + 1]
    k_scr, vt_scr, st_in, st_out, sem_in, sem_out = rest[2 * n_side + 1:]
    t_len = kl_ref.shape[0]
    n_keys = k_scr.shape[0]
    slab = st_in.shape[1]
    step = (pl.program_id(0) * pl.num_programs(1) + pl.program_id(1)) * pl.num_programs(2) + pl.program_id(2)
    last_step = pl.num_programs(0) * pl.num_programs(1) * pl.num_programs(2) - 1

    def slab_in(w, at_step):
        rows = pl.ds(pl.multiple_of(at_step * slab, slab), slab)
        return pltpu.make_async_copy(side_in[w].at[rows, :], st_in.at[w], sem_in.at[w])

    def slab_out(w, at_step):
        rows = pl.ds(pl.multiple_of(at_step * slab, slab), slab)
        return pltpu.make_async_copy(st_out.at[w], side_out[w].at[rows, :], sem_out.at[w])

    for w in range(n_side):
        slab_in(w, step).start()

    @pl.when(pl.program_id(2) == 0)
    def _():
        k_scr[0:t_len, :] = kl_ref[...]
        k_scr[t_len:, :] = kc_ref[...]
        vt_scr[0:HEAD_DIM, 0:t_len] = vl_ref[...].astype(F32).T.astype(BF16)
        vt_scr[0:HEAD_DIM, t_len:] = vc_ref[...].astype(F32).T.astype(BF16)
        vt_scr[HEAD_DIM:, :] = jnp.ones((vt_scr.shape[0] - HEAD_DIM, vt_scr.shape[1]), BF16)

    heads = [slice(g * HEAD_DIM, (g + 1) * HEAD_DIM) for g in range(GROUP)]
    chunks = [slice(c * key_chunk, (c + 1) * key_chunk) for c in range(n_keys // key_chunk)]

    def scores(g, ks):
        return _dot_nt(k_scr[ks, :], q_ref[:, heads[g]])

    def weigh(s_chunk, mx, ks):
        p = jnp.exp2(s_chunk - mx).astype(BF16)
        return _dot(vt_scr[:, ks], p)

    def col_max(s_chunks):
        part = [_col_reduce(sc, jnp.maximum) for sc in s_chunks]
        return jnp.max(functools.reduce(jnp.maximum, part), axis=0, keepdims=True)

    s_cur = [scores(0, ks) for ks in chunks]
    for g in range(GROUP):
        mx = col_max(s_cur)
        s_nxt, acc = [], None
        for c, ks in enumerate(chunks):
            if g + 1 < GROUP:
                s_nxt.append(scores(g + 1, ks))
            pv = weigh(s_cur[c], mx, ks)
            acc = pv if acc is None else acc + pv
        o = acc[0:HEAD_DIM] / acc[HEAD_DIM:HEAD_DIM + 1]
        o_ref[:, heads[g]] = (o.T * sg_ref[:, heads[g]].astype(F32)).astype(BF16)
        s_cur = s_nxt

    for w in range(n_side):
        slab_in(w, step).wait()

        @pl.when(step > 0)
        def _(w=w):
            slab_out(w, step - 1).wait()

        st_out[w] = st_in[w].astype(BF16)
        slab_out(w, step).start()

        @pl.when(step == last_step)
        def _(w=w):
            slab_out(w, step).wait()


def _attention(q, kk, vv, sg, batch, t_len, c_len, tq, side):
    m_lat = batch * t_len
    n_kv = kk.shape[1] // HEAD_DIM
    gw = GROUP * HEAD_DIM
    nq = t_len // tq
    ctx0 = m_lat // c_len
    n_keys = t_len + c_len
    half = n_keys // 2
    key_chunk = half if half % LANES == 0 else n_keys
    steps = batch * n_kv * nq
    rows, cols = side[0].shape
    assert all(w.shape == (rows, cols) for w in side) and rows % (steps * 2 * SUBLANES) == 0
    slab = rows // steps
    any_spec = pl.BlockSpec(memory_space=pl.ANY)
    return pl.pallas_call(
        functools.partial(_attn_kernel, key_chunk=key_chunk, n_side=len(side)),
        grid=(batch, n_kv, nq),
        in_specs=[pl.BlockSpec((tq, gw), lambda b, h, i: (b * nq + i, h)),
                  pl.BlockSpec((t_len, HEAD_DIM), lambda b, h, i: (b, h)),
                  pl.BlockSpec((t_len, HEAD_DIM), lambda b, h, i: (b, h)),
                  pl.BlockSpec((c_len, HEAD_DIM), lambda b, h, i: (ctx0 + b, h)),
                  pl.BlockSpec((c_len, HEAD_DIM), lambda b, h, i: (ctx0 + b, h)),
                  pl.BlockSpec((tq, gw), lambda b, h, i: (b * nq + i, h))] + [any_spec] * len(side),
        out_specs=[pl.BlockSpec((tq, gw), lambda b, h, i: (b * nq + i, h))] + [any_spec] * len(side),
        out_shape=[jax.ShapeDtypeStruct((m_lat, n_kv * gw), BF16)]
                  + [jax.ShapeDtypeStruct((rows, cols), BF16) for _ in side],
        scratch_shapes=[pltpu.VMEM((n_keys, HEAD_DIM), BF16),
                        pltpu.VMEM((HEAD_DIM + 2 * SUBLANES, n_keys), BF16),
                        pltpu.VMEM((len(side), slab, cols), F32),
                        pltpu.VMEM((len(side), slab, cols), BF16),
                        pltpu.SemaphoreType.DMA((len(side),)),
                        pltpu.SemaphoreType.DMA((len(side),))],
        compiler_params=_params(("arbitrary", "arbitrary", "arbitrary")),
        name="attn",
    )(q, kk, vv, kk, vv, sg, *side)


def _hgrn_masks(c, reverse):
    ns = c // SUB
    row = lax.broadcasted_iota(jnp.int32, (c, c), 0)
    col = lax.broadcasted_iota(jnp.int32, (c, c), 1)
    pr, pc = row // SUB, col // SUB
    if reverse:
        pr, pc = ns - 1 - pr, ns - 1 - pc
        causal = col >= row
    else:
        causal = col <= row
    m_diag = (pr == pc) & causal
    m_adj = ((pr == 1) & (pc == 0)) | ((pr == 3) & (pc == 2))
    m_far = (pr >= 2) & (pc <= 1)
    return m_diag, m_adj, m_far


def _hgrn_prepare(q, k, bfull, reverse, want_out):
    c = k.shape[0]
    ns = c // SUB
    assert ns == 4
    order = list(range(ns))[::-1] if reverse else list(range(ns))
    zero = jnp.zeros((1, LANES), F32)
    start, end = {}, {}
    for i in range(ns):
        lo, hi = i * SUB, i * SUB + SUB - 1
        if reverse:
            start[i] = bfull[hi + 1:hi + 2, :] if i < ns - 1 else zero
            end[i] = bfull[lo:lo + 1, :]
        else:
            start[i] = bfull[lo - 1:lo, :] if i > 0 else zero
            end[i] = bfull[hi:hi + 1, :]
    btot = end[order[-1]]
    qt, khat, kd, q_far, k_far, qs, ke = {}, {}, {}, {}, {}, {}, {}
    for p, i in enumerate(order):
        rs = slice(i * SUB, (i + 1) * SUB)
        b_i = bfull[rs]
        kd[i] = k[rs] * jnp.exp(end[i] - b_i)
        ke[i] = kd[i] if p == ns - 1 else kd[i] * jnp.exp(btot - end[i])
        if want_out:
            bl = b_i - start[i]
            qt[i] = q[rs] * jnp.exp(bl)
            khat[i] = k[rs] * jnp.exp(-bl)
            qs[i] = qt[i] if p == 0 else qt[i] * jnp.exp(start[i])
            q_far[i] = qt[i] * jnp.exp(start[i] - start[order[2]]) if p == 3 else qt[i]
            k_far[i] = kd[i] * jnp.exp(start[order[2]] - end[i]) if p == 0 else kd[i]

    def cat(d):
        return jnp.concatenate([d[i].astype(BF16) for i in range(ns)], axis=0)

    out = {"ke": cat(ke), "dec": jnp.exp(btot)}
    if want_out:
        out.update(qt=cat(qt), kk=jnp.concatenate([cat(kd), cat(khat)], axis=0),
                   q_far=cat(q_far), k_far=cat(k_far), qs=cat(qs))
    return out


def _hgrn_kernel(*refs, reverse, n_ctx, heads, final):
    if final:
        q_ref, k_ref, b_ref, v_ref, of_ref, sg_ref, hw_ref, o_ref, st_ref = refs
    else:
        q_ref, k_ref, b_ref, v_ref, o_ref, st_ref = refs
    s = pl.program_id(2)

    @pl.when(s == 0)
    def _():
        st_ref[...] = jnp.zeros_like(st_ref)

    c = HG_BLOCK
    n_sub = k_ref.shape[0] // c
    order = list(range(n_sub))[::-1] if reverse else list(range(n_sub))
    cols = [slice(j * LANES, (j + 1) * LANES) for j in range(heads)]

    def run(want_out):
        if want_out:
            m_diag, m_adj, m_far = _hgrn_masks(c, reverse)
        for bi in order:
            rows = slice(bi * c, (bi + 1) * c)
            prep = [_hgrn_prepare(q_ref[rows, cs].astype(F32) if want_out else None,
                                  k_ref[rows, cs].astype(F32), b_ref[rows, cs], reverse, want_out)
                    for cs in cols]
            if want_out:
                near = [_dot_nt(p["qt"], p["kk"]) for p in prep]
                far = [_dot_nt(p["q_far"], p["k_far"]) for p in prep]
                inter = [_dot_nt(p["qs"], st_ref[j].astype(BF16)) for j, p in enumerate(prep)]
            for j, (cs, p) in enumerate(zip(cols, prep)):
                st_ref[j] = p["dec"] * st_ref[j] + _dot_tn(v_ref[rows, cs], p["ke"])
            if want_out:
                amat = [jnp.where(m_diag, n[:, c:], jnp.where(m_adj, n[:, :c], jnp.where(m_far, f, 0.0))).astype(BF16)
                        for n, f in zip(near, far)]
                for j, cs in enumerate(cols):
                    o = _dot(amat[j], v_ref[rows, cs]) + inter[j]
                    if final:
                        tot = o + of_ref[rows, cs]
                        y = tot * lax.rsqrt(jnp.mean(tot * tot, axis=-1, keepdims=True) + EPS)
                        y = y * hw_ref[:, cs]
                        o_ref[rows, cs] = (y * sg_ref[rows, cs].astype(F32)).astype(o_ref.dtype)
                    else:
                        o_ref[rows, cs] = o.astype(o_ref.dtype)

    pl.when(s < n_ctx)(lambda: run(False))
    pl.when(s >= n_ctx)(lambda: run(True))


def _hgrn(q, kk, bb, vv, w, batch, t_len, c_len, reverse, heads, cols, blk, final_args=None):
    m_lat = batch * t_len
    hb = heads * LANES
    n_ctx, n_lat = c_len // blk, t_len // blk
    ctx0 = m_lat // blk
    q0, k0, v0, sg0 = (cc // hb for cc in cols)

    def row_block(b, s):
        if reverse:
            cblk = ctx0 + b * n_ctx + (n_ctx - 1 - s)
            lblk = b * n_lat + (n_lat - 1 - (s - n_ctx))
        else:
            cblk = ctx0 + b * n_ctx + s
            lblk = b * n_lat + (s - n_ctx)
        return jnp.where(s < n_ctx, cblk, lblk)

    def lat_block(b, s):
        sl = jnp.maximum(s - n_ctx, 0)
        return b * n_lat + ((n_lat - 1 - sl) if reverse else sl)

    def all_spec(c0):
        return pl.BlockSpec((blk, hb), lambda b, h, s: (row_block(b, s), c0 + h))

    def lat_spec(c0):
        return pl.BlockSpec((blk, hb), lambda b, h, s: (lat_block(b, s), c0 + h))

    in_specs = [lat_spec(q0), all_spec(k0), all_spec(k0), all_spec(v0)]
    args = [q, kk, bb, vv]
    final = final_args is not None
    if final:
        o_f, sg, hw = final_args
        in_specs += [lat_spec(0), lat_spec(sg0), pl.BlockSpec((1, hb), lambda b, h, s: (0, h))]
        args += [o_f, sg, hw]
    return pl.pallas_call(
        functools.partial(_hgrn_kernel, reverse=reverse, n_ctx=n_ctx, heads=heads, final=final),
        grid=(batch, w // hb, n_ctx + n_lat),
        in_specs=in_specs,
        out_specs=lat_spec(0),
        out_shape=jax.ShapeDtypeStruct((m_lat, w), BF16 if final else F32),
        scratch_shapes=[pltpu.VMEM((heads, LANES, LANES), F32)],
        compiler_params=_params(("arbitrary", "arbitrary", "arbitrary")),
        name="hgrn_bwd" if reverse else "hgrn_fwd",
    )(*args)


def _proj_kernel(yh_ref, ya_ref, wh_ref, wa_ref, rh_ref, ra_ref, o_ref):
    a = _dot(yh_ref[...], wh_ref[...])
    b = _dot(ya_ref[...], wa_ref[...])
    o_ref[...] = (rh_ref[...].astype(F32) * a + ra_ref[...].astype(F32) * b).astype(BF16)


def _proj(yh, ya, wh, wa, gates, rh_col0, ra_col0, tm, tn):
    m, k = yh.shape
    n = wh.shape[1]
    rh0, ra0 = rh_col0 // tn, ra_col0 // tn
    return pl.pallas_call(
        _proj_kernel,
        grid=(m // tm, n // tn),
        in_specs=[pl.BlockSpec((tm, k), lambda i, j: (i, 0)),
                  pl.BlockSpec((tm, k), lambda i, j: (i, 0)),
                  pl.BlockSpec((k, tn), lambda i, j: (0, j)),
                  pl.BlockSpec((k, tn), lambda i, j: (0, j)),
                  pl.BlockSpec((tm, tn), lambda i, j: (i, rh0 + j)),
                  pl.BlockSpec((tm, tn), lambda i, j: (i, ra0 + j))],
        out_specs=pl.BlockSpec((tm, tn), lambda i, j: (i, j)),
        out_shape=jax.ShapeDtypeStruct((m, n), BF16),
        compiler_params=_params(("arbitrary", "arbitrary")),
        name="proj",
    )(yh, ya, wh, wa, gates, gates)


def _out_kernel(mg_ref, w_ref, x_ref, gate_ref, fw_ref, o_ref, xn_scr):
    n = pl.program_id(1)
    n_tiles, _, tn = xn_scr.shape
    xn_scr[n] = x_ref[...] + gate_ref[0] * _dot(mg_ref[...], w_ref[...])

    @pl.when(n == n_tiles - 1)
    def _():
        ssq = None
        for j in range(n_tiles):
            v = xn_scr[j]
            part = jnp.sum(v * v, axis=-1, keepdims=True)
            ssq = part if ssq is None else ssq + part
        inv = lax.rsqrt(ssq / (n_tiles * tn) + EPS)
        for j in range(n_tiles):
            cs = slice(j * tn, (j + 1) * tn)
            o_ref[:, cs] = xn_scr[j] * inv * fw_ref[:, cs]


def _outproj(mg, w, x2, mod3, fw, t_len, tm, tn):
    m, k = mg.shape
    n = w.shape[1]
    gate0 = 2 * n // tn
    per_b = t_len // tm
    return pl.pallas_call(
        _out_kernel,
        grid=(m // tm, n // tn),
        in_specs=[pl.BlockSpec((tm, k), lambda i, j: (i, 0)),
                  pl.BlockSpec((k, tn), lambda i, j: (0, j)),
                  pl.BlockSpec((tm, tn), lambda i, j: (i, j)),
                  pl.BlockSpec((1, 1, tn), lambda i, j: (i // per_b, 0, gate0 + j)),
                  pl.BlockSpec((1, n), lambda i, j: (0, 0))],
        out_specs=pl.BlockSpec((tm, n), lambda i, j: (i, 0)),
        out_shape=jax.ShapeDtypeStruct((m, n), F32),
        scratch_shapes=[pltpu.VMEM((n // tn, tm, tn), F32)],
        compiler_params=_params(("arbitrary", "arbitrary")),
        name="outproj",
    )(mg, w, x2, mod3, fw)


def _rope_tables(t_len, pad_rows):
    axis_dim = HEAD_DIM // 2
    rows = t_len // GRID_W
    row = jnp.repeat(jnp.arange(rows, dtype=F32), GRID_W)
    col = jnp.tile(jnp.arange(GRID_W, dtype=F32), rows)
    inv = ROPE_THETA ** (-jnp.arange(0, axis_dim, 2, dtype=F32) / axis_dim)
    fr = row[:, None] * inv[None]
    fc = col[:, None] * inv[None]
    ang = jnp.concatenate([fr, fr, fc, fc], axis=-1)
    cos, sin = jnp.cos(ang), jnp.sin(ang)
    lane = jnp.arange(HEAD_DIM)
    sign = jnp.where((lane // (HEAD_DIM // 4)) % 2 == 0, -1.0, 1.0).astype(F32)
    cos = jnp.concatenate([cos, jnp.ones((pad_rows, HEAD_DIM), F32)], axis=0)
    sin = jnp.concatenate([sin * sign, jnp.zeros((pad_rows, HEAD_DIM), F32)], axis=0)
    return cos, sin


def _tile(n, pref):
    t = min(n, pref)
    while n % t:
        t //= 2
    return t


def kernel(x, c, ctx, c_ctx, norm_w, w_ada, b_ada, w_in, q_norm_w, k_norm_w, lb_logits,
           hg_norm_w, w_proj_hg, w_proj_att, w_out, final_norm_w):
    batch, t_len, d = x.shape
    c_len = ctx.shape[1]
    depth = norm_w.shape[0]
    assert depth == 1, "single-layer block"
    assert t_len % HG_BLOCK == 0 and c_len % HG_BLOCK == 0
    att_w = d
    kv_w = d // GROUP
    m_lat, m_ctx = batch * t_len, batch * c_len
    m_all = m_lat + m_ctx

    x2 = x.reshape(m_lat, d)
    c2 = ctx.reshape(m_ctx, d)

    pad = (-(batch + 1)) % SUBLANES
    s = jnp.concatenate([c, c_ctx[None, :], jnp.zeros((pad, d), F32)], axis=0)
    mod = _ada(s, w_ada[0], b_ada[0][None, :], _tile(3 * d, 512))
    mod3 = mod.reshape(mod.shape[0], 1, 3 * d)

    h = _prenorm(x2, c2, norm_w[0][None, :], mod3, batch, _tile(math.gcd(m_ctx, t_len), 512))

    tm = _tile(m_ctx, 1024)
    tn = _tile(kv_w, 1024)
    n_lat_tiles = m_lat // tm
    per_seq = t_len // tm
    cos, sin = _rope_tables(t_len, tm)
    sw = min(MXU_N, tn)
    lane = jnp.arange(sw)
    perm = (lane[:, None] == (lane[None, :] ^ (HEAD_DIM // 4))).astype(BF16)
    perm2 = jnp.concatenate([perm, perm], axis=0)
    partner = jnp.arange(HEAD_DIM) ^ (HEAD_DIM // 4)

    def rope_inputs(nw):
        return [cos * nw[None, :], sin * nw[partner][None, :], perm2]

    def rope_row(n, m):
        return (jnp.where(m < n_lat_tiles, m % per_seq, per_seq), 0)

    rope_specs = [((tm, HEAD_DIM), rope_row), ((tm, HEAD_DIM), rope_row), ((2 * sw, sw), lambda n, m: (0, 0))]
    w = w_in[0]
    c_k, c_v, c_f, c_q, c_g = 0, kv_w, 2 * kv_w + d, 2 * kv_w + 3 * d, 2 * kv_w + 3 * d + att_w

    (k_att,) = _inproj(h, w, c_k, kv_w, m_all, functools.partial(_ep_normrope, post_scale=1.0),
                       rope_inputs(k_norm_w[0]), rope_specs, [BF16], tm, tn, "inproj_k")
    (v_att,) = _inproj(h, w, c_v, kv_w, m_all, _ep_plain, [], [], [BF16], tm, tn, "inproj_v")
    (i_hg,) = _inproj(h, w, c_v + kv_w, d, m_all, _ep_plain, [], [], [BF16], tm, tn, "inproj_i")
    slots = lb_logits.shape[1]
    lb2 = jnp.transpose(lb_logits, (1, 0, 2)).reshape(slots, 2 * d)
    b_all, k_all = _inproj(h, w, c_f, 2 * d, m_all, functools.partial(_ep_forget, n_fwd=d // tn),
                           [lb2], [((slots, tn), lambda n, m: (0, n))], [F32, BF16], tm, tn, "inproj_f")
    (q_att,) = _inproj(h, w, c_q, att_w, m_lat,
                       functools.partial(_ep_normrope, post_scale=HEAD_DIM ** -0.5 * LOG2E),
                       rope_inputs(q_norm_w[0]), rope_specs, [BF16], tm, tn, "inproj_q")
    (gates,) = _inproj(h, w, c_g, att_w + 4 * d, m_lat,
                       functools.partial(_ep_gate, n_silu=(att_w + 2 * d) // tn),
                       [], [], [BF16], tm, tn, "inproj_g")
    g_qh, g_gh, g_rh, g_ra = att_w, att_w + d, att_w + 2 * d, att_w + 3 * d

    y_att, wh, wa, wo = _attention(q_att, k_att, v_att, gates, batch, t_len, c_len, _tile(t_len, 1024),
                                   [w_proj_hg[0], w_proj_att[0], w_out[0]])

    heads = _tile(d // LANES, 32)
    blk = _tile(c_len, 2 * HG_BLOCK)
    o_f = _hgrn(gates, k_all, b_all, i_hg, d, batch, t_len, c_len, False, heads, (g_qh, 0, 0, 0), blk)
    y_hg = _hgrn(gates, k_all, b_all, i_hg, d, batch, t_len, c_len, True, heads, (g_qh, d, 0, g_gh), blk,
                 final_args=(o_f, gates, hg_norm_w[0][None, :]))

    tm2 = _tile(t_len, 512)
    tn2 = _tile(d, 512)
    merged = _proj(y_hg, y_att, wh, wa, gates, g_rh, g_ra, tm2, tn2)
    out = _outproj(merged, wo, x2, mod3, final_norm_w[None, :], t_len, tm2, tn2)
    return out.reshape(batch, t_len, d)
```
